```python
import jax, jax.numpy as jnp
from jax import lax
import numpy as np

D_MODEL = 4096
BATCH = 2
SEQ = 4096
DEPTH = 2

HEAD_DIM_A = 128
N_HEADS_A = D_MODEL // 256
A_WIDTH = N_HEADS_A * HEAD_DIM_A
DILATED_PATTERNS = ((128, 1), (512, 4), (2048, 16))
ROPE_THETA = 10000.0
DK_B = 128
DV_B = 128
N_HEADS_B = D_MODEL // 256
B_KEY = N_HEADS_B * DK_B
B_VAL = N_HEADS_B * DV_B
HGRN_CHUNK = 64
IN_COLS = 3 * A_WIDTH + 2 * B_KEY + 2 * B_VAL + 2 * D_MODEL
N_EXPERTS = 64
TOP_K = 8
N_GROUPS = 8
TOPK_GROUPS = 4
D_EXPERT = D_MODEL // 16
D_SHARED = D_MODEL // 16
ROUTED_SCALE = 2.5
N_MOD = 6
EPS = 1e-6

kernel_name = 'hybrid_dilated_attn_hgrn2_moe_adaln'


def rmsnorm(x, w):
    xf = x.astype(jnp.float32)
    y = xf * lax.rsqrt(jnp.mean(xf * xf, axis=-1, keepdims=True) + EPS)
    return (y * w.astype(jnp.float32)).astype(x.dtype)


def modulate(h, shift, scale):
    return h * (1 + scale[:, None, :]) + shift[:, None, :]


def rope(t, pos):
    half = t.shape[-1] // 2
    inv = ROPE_THETA ** (-jnp.arange(half, dtype=jnp.float32) / half)
    ang = pos.astype(jnp.float32)[:, None] * inv[None, :]
    cos = jnp.cos(ang)[None, :, None, :]
    sin = jnp.sin(ang)[None, :, None, :]
    t1, t2 = t[..., :half], t[..., half:]
    return jnp.concatenate([t1 * cos - t2 * sin, t2 * cos + t1 * sin], axis=-1)


def dilated_window_attention(q, k, v, window, dilation):
    B, S, H, hd = q.shape
    span = window // dilation
    L = S // dilation
    nb = -(-L // span)
    Lp = nb * span

    def to_sub(t):
        t = t.reshape(B, L, dilation, H, hd).transpose(0, 2, 1, 3, 4)
        return jnp.pad(t, ((0, 0), (0, 0), (0, Lp - L), (0, 0), (0, 0)))

    def kv_blocks(t):
        t = jnp.pad(to_sub(t), ((0, 0), (0, 0), (span, 0), (0, 0), (0, 0)))
        t = t.reshape(B, dilation, nb + 1, span, H, hd)
        return jnp.concatenate([t[:, :, :-1], t[:, :, 1:]], axis=3)

    qb = to_sub(q).reshape(B, dilation, nb, span, H, hd)
    kb, vb = kv_blocks(k), kv_blocks(v)
    i = jnp.arange(span)[:, None]
    j = jnp.arange(2 * span)[None, :]
    blk = jnp.arange(nb)[:, None, None]
    mask = (j >= i) & (j <= i + span) & (blk * span + j - span >= 0)
    s = jnp.einsum('brnqhd,brnkhd->brnhqk', qb, kb) * (hd ** -0.5)
    s = jnp.where(mask[None, None, :, None], s, -jnp.inf)
    m = jnp.max(s, axis=-1, keepdims=True)
    p = jnp.exp(s - m)
    den = jnp.sum(p, axis=-1)
    o = jnp.einsum('brnhqk,brnkhd->brnqhd', p, vb) / den.transpose(0, 1, 2, 4, 3)[..., None]
    lse = (m[..., 0] + jnp.log(den)).transpose(0, 1, 2, 4, 3)

    def from_sub(t):
        t = t.reshape((B, dilation, Lp) + t.shape[4:])[:, :, :L]
        return jnp.swapaxes(t, 1, 2).reshape((B, S) + t.shape[3:])

    return from_sub(o), from_sub(lse)


def hgrn2_chunked(q, k, v, log_f):
    B, S, H, DK = q.shape
    DV = v.shape[-1]
    C = HGRN_CHUNK
    n = S // C

    def chunks(t):
        return t.reshape(B, n, C, H, t.shape[-1]).transpose(1, 0, 3, 2, 4)

    causal = jnp.tril(jnp.ones((C, C), dtype=bool))

    def step(state, inp):
        qi, ki, vi, gi = inp
        b = jnp.cumsum(gi, axis=2)
        o_inter = jnp.einsum('bhtd,bhde->bhte', qi * jnp.exp(b), state)
        rel = b[:, :, :, None, :] - b[:, :, None, :, :]
        decay = jnp.exp(jnp.where(causal[:, :, None], rel, -jnp.inf))
        scores = jnp.einsum('bhtd,bhtsd,bhsd->bhts', qi, decay, ki)
        o_intra = jnp.einsum('bhts,bhse->bhte', scores, vi)
        b_last = b[:, :, -1:, :]
        new_state = jnp.exp(b_last[:, :, 0, :])[..., None] * state + jnp.einsum(
            'bhsd,bhse->bhde', ki * jnp.exp(b_last - b), vi)
        return new_state, o_inter + o_intra

    state0 = jnp.zeros((B, H, DK, DV), jnp.float32)
    _, o = lax.scan(step, state0, (chunks(q), chunks(k), chunks(v), chunks(log_f)))
    return o.transpose(1, 0, 3, 2, 4).reshape(B, S, H, DV)


def hybrid_mixer(h, w_in, lb, out_norm, w_proj_a, w_proj_b, w_out):
    B, S, _ = h.shape
    sizes = (A_WIDTH,) * 3 + (B_KEY,) * 2 + (B_VAL,) * 2 + (D_MODEL,) * 2
    points = np.cumsum(sizes)[:-1].tolist()
    q_a, k_a, v_a, q_b, f_b, i_b, g_b, gate_a, gate_b = jnp.split(h @ w_in, points, axis=-1)

    def heads(t, d):
        return t.reshape(B, S, -1, d).astype(jnp.float32)

    pos = jnp.arange(S)
    qa = rope(heads(q_a, HEAD_DIM_A), pos)
    ka = rope(heads(k_a, HEAD_DIM_A), pos)
    va = heads(v_a, HEAD_DIM_A)
    outs, lses = [], []
    for window, dilation in DILATED_PATTERNS:
        o_g, lse_g = dilated_window_attention(qa, ka, va, window, dilation)
        outs.append(o_g)
        lses.append(lse_g)
    wts = jax.nn.softmax(jnp.stack(lses, axis=0), axis=0)
    o_a = jnp.einsum('gbsh,gbshd->bshd', wts, jnp.stack(outs, axis=0))
    o_a = o_a.reshape(B, S, A_WIDTH).astype(h.dtype)

    f = lb.astype(jnp.float32) + (1 - lb.astype(jnp.float32)) * jax.nn.sigmoid(f_b.astype(jnp.float32))
    o = hgrn2_chunked(heads(jax.nn.silu(q_b), DK_B), heads(1 - f, DK_B),
                      heads(i_b, DV_B), heads(jnp.log(f), DK_B))
    o = rmsnorm(o, out_norm) * jax.nn.silu(heads(g_b, DV_B))
    o_b = o.reshape(B, S, B_VAL).astype(h.dtype)

    u = jax.nn.sigmoid(gate_a) * (o_a @ w_proj_a) + jax.nn.sigmoid(gate_b) * (o_b @ w_proj_b)
    return u @ w_out


def moe_ffn(h, w_router, router_bias, w_exp_gate, w_exp_up, w_exp_down, w_sh_gate, w_sh_up, w_sh_down):
    B, S, D = h.shape
    t = h.reshape(B * S, D)
    scores = jax.nn.sigmoid((t @ w_router).astype(jnp.float32))
    sel = scores + router_bias.astype(jnp.float32)
    grp = sel.reshape(-1, N_GROUPS, N_EXPERTS // N_GROUPS)
    grp_score = jnp.sum(lax.top_k(grp, 2)[0], axis=-1)
    _, top_g = lax.top_k(grp_score, TOPK_GROUPS)
    gmask = jnp.any(top_g[..., None] == jnp.arange(N_GROUPS), axis=1)
    emask = jnp.repeat(gmask, N_EXPERTS // N_GROUPS, axis=1)
    _, idx = lax.top_k(jnp.where(emask, sel, -jnp.inf), TOP_K)
    w = jnp.take_along_axis(scores, idx, axis=1)
    w = w / jnp.sum(w, axis=-1, keepdims=True) * ROUTED_SCALE
    combine = jnp.sum((idx[..., None] == jnp.arange(N_EXPERTS)) * w[..., None], axis=1)
    gate = jnp.einsum('td,edf->tef', t, w_exp_gate)
    up = jnp.einsum('td,edf->tef', t, w_exp_up)
    act = jax.nn.silu(gate) * up * combine[..., None].astype(t.dtype)
    routed = jnp.einsum('tef,efd->td', act, w_exp_down)
    shared = (jax.nn.silu(t @ w_sh_gate) * (t @ w_sh_up)) @ w_sh_down
    return (routed + shared).reshape(B, S, D)


def setup_inputs(seed: int = 0) -> dict:
    key = jax.random.key(seed)
    ks = jax.random.split(key, 21)
    D = D_MODEL
    f32 = jnp.float32

    def nrm(k, shape, fan_in, s=1.0):
        return jax.random.normal(k, shape, f32) * (s * fan_in ** -0.5)

    def gain(k, shape):
        return 1.0 + 0.05 * jax.random.normal(k, shape, f32)

    return {
        'x': jax.random.normal(ks[0], (BATCH, SEQ, D), f32),
        'c': jax.random.normal(ks[1], (BATCH, D), f32),
        'norm_mix': gain(ks[2], (DEPTH, D)),
        'norm_ffn': gain(ks[3], (DEPTH, D)),
        'w_ada': nrm(ks[4], (DEPTH, D, N_MOD * D), D, 0.5),
        'b_ada': 0.02 * jax.random.normal(ks[5], (DEPTH, N_MOD * D), f32),
        'w_in': nrm(ks[6], (DEPTH, D, IN_COLS), D),
        'hgrn_lower_bounds': jax.random.normal(ks[7], (DEPTH, B_KEY), f32),
        'hgrn_out_norm': gain(ks[8], (DEPTH, DV_B)),
        'w_proj_a': nrm(ks[9], (DEPTH, A_WIDTH, D), A_WIDTH),
        'w_proj_b': nrm(ks[10], (DEPTH, B_VAL, D), B_VAL),
        'w_out': nrm(ks[11], (DEPTH, D, D), D),
        'w_router': nrm(ks[12], (DEPTH, D, N_EXPERTS), D),
        'router_bias': 0.01 * jax.random.normal(ks[13], (DEPTH, N_EXPERTS), f32),
        'w_exp_gate': nrm(ks[14], (DEPTH, N_EXPERTS, D, D_EXPERT), D),
        'w_exp_up': nrm(ks[15], (DEPTH, N_EXPERTS, D, D_EXPERT), D),
        'w_exp_down': nrm(ks[16], (DEPTH, N_EXPERTS, D_EXPERT, D), D_EXPERT),
        'w_sh_gate': nrm(ks[17], (DEPTH, D, D_SHARED), D),
        'w_sh_up': nrm(ks[18], (DEPTH, D, D_SHARED), D),
        'w_sh_down': nrm(ks[19], (DEPTH, D_SHARED, D), D_SHARED),
        'norm_final': gain(ks[20], (D,)),
    }


def reference(x, c, norm_mix, norm_ffn, w_ada, b_ada, w_in, hgrn_lower_bounds, hgrn_out_norm,
              w_proj_a, w_proj_b, w_out, w_router, router_bias, w_exp_gate, w_exp_up, w_exp_down,
              w_sh_gate, w_sh_up, w_sh_down, norm_final):
    lb_sm = jax.nn.softmax(hgrn_lower_bounds.astype(jnp.float32), axis=0)
    lb_all = jnp.cumsum(lb_sm, axis=0) - lb_sm[0:1]
    c_act = jax.nn.silu(c)
    for l in range(DEPTH):
        mod = c_act @ w_ada[l] + b_ada[l]
        sh1, sc1, g1, sh2, sc2, g2 = jnp.split(mod, N_MOD, axis=-1)
        h = modulate(rmsnorm(x, norm_mix[l]), sh1, sc1)
        y = hybrid_mixer(h, w_in[l], lb_all[l], hgrn_out_norm[l], w_proj_a[l], w_proj_b[l], w_out[l])
        x = x + (g1[:, None, :] * y).astype(x.dtype)
        h = modulate(rmsnorm(x, norm_ffn[l]), sh2, sc2)
        y = moe_ffn(h, w_router[l], router_bias[l], w_exp_gate[l], w_exp_up[l], w_exp_down[l],
                    w_sh_gate[l], w_sh_up[l], w_sh_down[l])
        x = x + (g2[:, None, :] * y).astype(x.dtype)
    return rmsnorm(x, norm_final)
```

```python
import functools
import math

import jax
import jax.numpy as jnp
import numpy as np
from jax import lax
from jax.experimental import pallas as pl
from jax.experimental.pallas import tpu as pltpu

HEAD_DIM = 128
DILATED_PATTERNS = ((128, 1), (512, 4), (2048, 16))
SPAN = 128
ROPE_THETA = 10000.0
N_EXPERTS = 64
TOP_K = 8
N_GROUPS = 8
TOPK_GROUPS = 4
GROUP_SIZE = N_EXPERTS // N_GROUPS
ROUTED_SCALE = 2.5
N_MOD = 6
EPS = 1e-6

LANES = 128
SUBLANES = 8
VMEM_LIMIT_BYTES = 56 * 1024 * 1024

HGRN_CHUNK = 128
HGRN_LEVELS = 7

BF16 = jnp.bfloat16
F32 = jnp.float32


def _params(*sem):
    return pltpu.CompilerParams(dimension_semantics=sem, vmem_limit_bytes=VMEM_LIMIT_BYTES)


def _sigmoid(x):
    return 1.0 / (1.0 + jnp.exp(-x))


def _silu(x):
    return x * _sigmoid(x)


def _split_bf16(x):
    hi = x.astype(BF16)
    lo = (x - hi.astype(F32)).astype(BF16)
    return hi, lo


def _adaln_kernel(cb_ref, w_ref, bias_ref, out_ref, cs_ref, *, kc):
    K, bn = w_ref.shape
    nb = cb_ref.shape[0]

    @pl.when((pl.program_id(0) == 0) & (pl.program_id(1) == 0))
    def _():
        cs_ref[...] = _silu(cb_ref[...])

    for j in range(bn // LANES):
        cols = slice(j * LANES, (j + 1) * LANES)

        def body(i, accs):
            k0 = pl.multiple_of(i * kc, kc)
            w = w_ref[pl.ds(k0, kc), cols]
            out = []
            for b in range(nb):
                p = (w * cs_ref[b, pl.ds(k0, kc), :]).reshape(kc // SUBLANES, SUBLANES, LANES)
                out.append(accs[b] + jnp.sum(p, axis=0))
            return tuple(out)

        accs = lax.fori_loop(0, K // kc, body,
                             tuple(jnp.zeros((SUBLANES, LANES), F32) for _ in range(nb)))
        for b in range(nb):
            out_ref[b:b + 1, cols] = jnp.sum(accs[b], axis=0, keepdims=True) + bias_ref[:, cols]


def adaln_modulation(c, w_ada, b_ada):
    nl, K, N = w_ada.shape
    nb = c.shape[0]
    bn = min(512, N)
    kc = min(64, K)
    cb = jnp.broadcast_to(c[:, :, None], (nb, K, LANES))
    return pl.pallas_call(
        functools.partial(_adaln_kernel, kc=kc),
        out_shape=jax.ShapeDtypeStruct((nl, nb, N), F32),
        grid=(nl, N // bn),
        in_specs=[
            pl.BlockSpec((nb, K, LANES), lambda l, n: (0, 0, 0)),
            pl.BlockSpec((None, K, bn), lambda l, n: (l, 0, n)),
            pl.BlockSpec((None, 1, bn), lambda l, n: (l, 0, n)),
        ],
        out_specs=pl.BlockSpec((None, nb, bn), lambda l, n: (l, 0, n)),
        scratch_shapes=[pltpu.VMEM((nb, K, LANES), F32)],
        compiler_params=_params("arbitrary", "arbitrary"),
        name="adaln_modulation",
    )(cb, w_ada, b_ada.reshape(nl, 1, N))


def _norm_mod(x, w, shift, scale):
    y = x * lax.rsqrt(jnp.mean(x * x, axis=-1, keepdims=True) + EPS) * w
    return y * (1.0 + scale) + shift


def _norm_mod_kernel(x_ref, w_ref, shift_ref, scale_ref, out_ref):
    out_ref[...] = _norm_mod(x_ref[...], w_ref[...], shift_ref[...], scale_ref[...]).astype(out_ref.dtype)


def _mod_spec(D, which, layer, nb):
    return pl.BlockSpec((None, 1, D), lambda b, *_: ((layer * nb + b) * N_MOD + which, 0, 0))


def norm_modulate(x, norm_w, mod_rows, layer, which_shift, which_scale):
    nb, S, D = x.shape
    ts = min(512, S)
    return pl.pallas_call(
        _norm_mod_kernel,
        out_shape=jax.ShapeDtypeStruct((nb, S, D), BF16),
        grid=(nb, S // ts),
        in_specs=[
            pl.BlockSpec((None, ts, D), lambda b, s: (b, s, 0)),
            pl.BlockSpec((None, 1, D), lambda b, s: (layer, 0, 0)),
            _mod_spec(D, which_shift, layer, nb),
            _mod_spec(D, which_scale, layer, nb),
        ],
        out_specs=pl.BlockSpec((None, ts, D), lambda b, s: (b, s, 0)),
        compiler_params=_params("parallel", "parallel"),
        name="norm_modulate",
    )(x, norm_w, mod_rows, mod_rows)


def _matmul_kernel(x_ref, w_ref, out_ref):
    out_ref[...] = jnp.dot(x_ref[...], w_ref[...], preferred_element_type=F32).astype(out_ref.dtype)


def matmul(x, w, out_dtype):
    M, K = x.shape
    N = w.shape[1]
    bm, bn = min(1024, M), min(1024, N)
    return pl.pallas_call(
        _matmul_kernel,
        out_shape=jax.ShapeDtypeStruct((M, N), out_dtype),
        grid=(M // bm, N // bn),
        in_specs=[pl.BlockSpec((bm, K), lambda m, n: (m, 0)),
                  pl.BlockSpec((K, bn), lambda m, n: (0, n))],
        out_specs=pl.BlockSpec((bm, bn), lambda m, n: (m, n)),
        compiler_params=_params("parallel", "parallel"),
        name="matmul",
    )(x, w)


def _rope_kernel(x_ref, cos_ref, sin_ref, out_ref):
    cos = cos_ref[...]
    sin = sin_ref[...]
    for h in range(x_ref.shape[-1] // HEAD_DIM):
        cols = slice(h * HEAD_DIM, (h + 1) * HEAD_DIM)
        t = x_ref[:, cols].astype(F32)
        out_ref[:, cols] = (t * cos + pltpu.roll(t, HEAD_DIM // 2, 1) * sin).astype(out_ref.dtype)


def rope_tables(S):
    half = HEAD_DIM // 2
    inv = ROPE_THETA ** (-jnp.arange(half, dtype=F32) / half)
    ang = jnp.arange(S, dtype=F32)[:, None] * inv[None, :]
    cos, sin = jnp.cos(ang), jnp.sin(ang)
    return jnp.concatenate([cos, cos], axis=-1), jnp.concatenate([-sin, sin], axis=-1)


def rope_qk(proj, width, cos, sin):
    nb, S, _ = proj.shape
    ts = min(512, S)
    return pl.pallas_call(
        _rope_kernel,
        out_shape=jax.ShapeDtypeStruct((nb, S, width), BF16),
        grid=(nb, S // ts),
        in_specs=[pl.BlockSpec((None, ts, width), lambda b, s: (b, s, 0)),
                  pl.BlockSpec((ts, HEAD_DIM), lambda b, s: (s, 0)),
                  pl.BlockSpec((ts, HEAD_DIM), lambda b, s: (s, 0))],
        out_specs=pl.BlockSpec((None, ts, width), lambda b, s: (b, s, 0)),
        compiler_params=_params("parallel", "parallel"),
        name="rope_qk",
    )(proj, cos, sin)


def _attn_kernel(q_ref, kp_ref, kc_ref, vp_ref, vc_ref, o_ref, lse_ref):
    i = pl.program_id(2)
    n_heads = q_ref.shape[-1] // HEAD_DIM
    qi = lax.broadcasted_iota(jnp.int32, (SPAN, SPAN), 0)
    kj = lax.broadcasted_iota(jnp.int32, (SPAN, SPAN), 1)
    mask_prev = (kj >= qi) & (i > 0)
    mask_cur = kj <= qi
    scale = HEAD_DIM ** -0.5
    nt = (((1,), (1,)), ((), ()))
    lses = []
    for h in range(n_heads):
        cols = slice(h * HEAD_DIM, (h + 1) * HEAD_DIM)
        q = q_ref[:, cols]
        sp = lax.dot_general(q, kp_ref[:, cols], nt, preferred_element_type=F32) * scale
        sc = lax.dot_general(q, kc_ref[:, cols], nt, preferred_element_type=F32) * scale
        sp = jnp.where(mask_prev, sp, -jnp.inf)
        sc = jnp.where(mask_cur, sc, -jnp.inf)
        m = jnp.maximum(jnp.max(sp, axis=-1, keepdims=True), jnp.max(sc, axis=-1, keepdims=True))
        pp = jnp.exp(sp - m)
        pc = jnp.exp(sc - m)
        den = jnp.sum(pp, axis=-1, keepdims=True) + jnp.sum(pc, axis=-1, keepdims=True)
        o = (jnp.dot(pp.astype(BF16), vp_ref[:, cols], preferred_element_type=F32)
             + jnp.dot(pc.astype(BF16), vc_ref[:, cols], preferred_element_type=F32))
        o_ref[:, cols] = (o / den).astype(o_ref.dtype)
        lses.append(m + jnp.log(den))
    lse_ref[...] = jnp.concatenate(lses, axis=-1)


def dilated_attention_branch(qk, proj, v_col0, width, dilation):
    nb, S, C = proj.shape
    L = S // dilation
    nblk = L // SPAN
    n_heads = width // HEAD_DIM
    qk_v = qk.reshape(nb, L, dilation * 2 * width)
    proj_v = proj.reshape(nb, L, dilation * C)
    vblk = v_col0 // width
    cper = C // width
    o, lse = pl.pallas_call(
        _attn_kernel,
        out_shape=(jax.ShapeDtypeStruct((nb, L, dilation * width), BF16),
                   jax.ShapeDtypeStruct((nb, dilation, L, n_heads), F32)),
        grid=(nb, dilation, nblk),
        in_specs=[
            pl.BlockSpec((None, SPAN, width), lambda b, r, i: (b, i, 2 * r)),
            pl.BlockSpec((None, SPAN, width), lambda b, r, i: (b, jnp.maximum(i - 1, 0), 2 * r + 1)),
            pl.BlockSpec((None, SPAN, width), lambda b, r, i: (b, i, 2 * r + 1)),
            pl.BlockSpec((None, SPAN, width), lambda b, r, i: (b, jnp.maximum(i - 1, 0), r * cper + vblk)),
            pl.BlockSpec((None, SPAN, width), lambda b, r, i: (b, i, r * cper + vblk)),
        ],
        out_specs=(pl.BlockSpec((None, SPAN, width), lambda b, r, i: (b, i, r)),
                   pl.BlockSpec((None, None, SPAN, n_heads), lambda b, r, i: (b, r, i, 0))),
        compiler_params=_params("parallel", "parallel", "arbitrary"),
        name=f"dilated_attention_d{dilation}",
    )(qk_v, qk_v, qk_v, proj_v, proj_v)
    o = o.reshape(nb, S, width)
    lse = jnp.transpose(lse, (0, 2, 1, 3)).reshape(nb, S, n_heads)
    return o, lse


def _attn_merge_kernel(*refs):
    n = (len(refs) - 1) // 2
    o_refs, lse_refs, out_ref = refs[:n], refs[n:2 * n], refs[-1]
    lses = [r[...] for r in lse_refs]
    m = functools.reduce(jnp.maximum, lses)
    es = [jnp.exp(l - m) for l in lses]
    tot = functools.reduce(jnp.add, es)
    ws = [e / tot for e in es]
    ts = out_ref.shape[0]
    for h in range(out_ref.shape[-1] // HEAD_DIM):
        cols = slice(h * HEAD_DIM, (h + 1) * HEAD_DIM)
        acc = jnp.zeros((ts, HEAD_DIM), F32)
        for g in range(n):
            acc = acc + ws[g][:, h:h + 1] * o_refs[g][:, cols].astype(F32)
        out_ref[:, cols] = acc.astype(out_ref.dtype)


def attention_merge(outs, lses):
    nb, S, width = outs[0].shape
    n_heads = lses[0].shape[-1]
    ts = min(512, S)
    o_spec = pl.BlockSpec((None, ts, width), lambda b, s: (b, s, 0))
    l_spec = pl.BlockSpec((None, ts, n_heads), lambda b, s: (b, s, 0))
    return pl.pallas_call(
        _attn_merge_kernel,
        out_shape=jax.ShapeDtypeStruct((nb, S, width), BF16),
        grid=(nb, S // ts),
        in_specs=[o_spec] * len(outs) + [l_spec] * len(lses),
        out_specs=o_spec,
        compiler_params=_params("parallel", "parallel"),
        name="attention_merge",
    )(*outs, *lses)


def _hgrn_sum_matrix(C, levels):
    t = np.arange(C)[:, None]
    u = np.arange(C)[None, :]
    blocks = [(u <= t), (u > t)]
    for j in range(levels):
        half = C >> (j + 1)
        mid = (t // (2 * half)) * (2 * half) + half - 1
        upper = (t // half) % 2 == 1
        blocks.append(np.where(upper, (u > mid) & (u <= t), (u > t) & (u <= mid)))
    return np.concatenate(blocks, axis=0).astype(np.float32)


def _hgrn_kernel(q_ref, f_ref, i_ref, g_ref, lb_ref, nw_ref, sm_ref, out_ref, state_ref, *, n_chunks):
    C = HGRN_CHUNK

    @pl.when(pl.program_id(2) == 0)
    def _():
        state_ref[...] = jnp.zeros_like(state_ref)

    ti = lax.broadcasted_iota(jnp.int32, (C, C), 0)
    si = lax.broadcasted_iota(jnp.int32, (C, C), 1)
    xor = ti ^ si
    lower = si < ti
    lb = lb_ref[...]
    nw = nw_ref[...]
    ones = jnp.ones((C, LANES), BF16)
    nt = (((1,), (1,)), ((), ()))
    tn = (((0,), (0,)), ((), ()))

    for c in range(n_chunks):
        rows = slice(c * C, (c + 1) * C)
        q = _silu(q_ref[rows, :].astype(F32))
        f = lb + (1.0 - lb) * _sigmoid(f_ref[rows, :].astype(F32))
        k = 1.0 - f
        g = jnp.log(f)
        v = i_ref[rows, :]
        g_hi, g_lo = _split_bf16(g)
        ghl = jnp.concatenate([g_hi, g_lo], axis=-1)
        e2 = jnp.dot(sm_ref[...], ghl, preferred_element_type=F32)
        e = e2[:, :HEAD_DIM] + e2[:, HEAD_DIM:]
        b_end = (lax.dot_general(g_hi, ones, tn, preferred_element_type=F32)
                 + lax.dot_general(g_lo, ones, tn, preferred_element_type=F32))

        scores = jnp.where(ti == si, lax.dot_general(q.astype(BF16), k.astype(BF16), nt,
                                                     preferred_element_type=F32), 0.0)
        for j in range(HGRN_LEVELS):
            a = jnp.exp(e[(2 + j) * C:(3 + j) * C, :])
            s_j = lax.dot_general((q * a).astype(BF16), (k * a).astype(BF16), nt,
                                  preferred_element_type=F32)
            scores = scores + jnp.where(lower & ((xor >> (HGRN_LEVELS - 1 - j)) == 1), s_j, 0.0)

        state = state_ref[...]
        o = jnp.dot(scores.astype(BF16), v, preferred_element_type=F32)
        o = o + jnp.dot((q * jnp.exp(e[0:C, :])).astype(BF16), state.astype(BF16),
                        preferred_element_type=F32)
        k_end = (k * jnp.exp(e[C:2 * C, :])).astype(BF16)
        state_ref[...] = jnp.exp(b_end) * state + lax.dot_general(k_end, v, tn, preferred_element_type=F32)

        y = o * lax.rsqrt(jnp.mean(o * o, axis=-1, keepdims=True) + EPS) * nw
        out_ref[rows, :] = (y * _silu(g_ref[rows, :].astype(F32))).astype(out_ref.dtype)


def hgrn2(proj, col_q, col_f, col_i, col_g, width, lb, out_norm):
    nb, S, _ = proj.shape
    n_heads = width // HEAD_DIM
    C = HGRN_CHUNK
    ts = min(512, S)
    sm = jnp.asarray(_hgrn_sum_matrix(C, HGRN_LEVELS), dtype=BF16)

    def col_spec(col0):
        blk = col0 // HEAD_DIM
        return pl.BlockSpec((None, ts, HEAD_DIM), lambda b, h, s: (b, s, blk + h))

    return pl.pallas_call(
        functools.partial(_hgrn_kernel, n_chunks=ts // C),
        out_shape=jax.ShapeDtypeStruct((nb, S, width), BF16),
        grid=(nb, n_heads, S // ts),
        in_specs=[col_spec(col_q), col_spec(col_f), col_spec(col_i), col_spec(col_g),
                  pl.BlockSpec((1, HEAD_DIM), lambda b, h, s: (0, h)),
                  pl.BlockSpec((1, HEAD_DIM), lambda b, h, s: (0, 0)),
                  pl.BlockSpec(sm.shape, lambda b, h, s: (0, 0))],
        out_specs=pl.BlockSpec((None, ts, HEAD_DIM), lambda b, h, s: (b, s, h)),
        scratch_shapes=[pltpu.VMEM((HEAD_DIM, HEAD_DIM), F32)],
        compiler_params=_params("parallel", "parallel", "arbitrary"),
        name="hgrn2",
    )(proj, proj, proj, proj, lb, out_norm, sm)


def _merge_proj_kernel(oa_ref, ob_ref, wa_ref, wb_ref, ga_ref, gb_ref, out_ref):
    ya = jnp.dot(oa_ref[...], wa_ref[...], preferred_element_type=F32)
    yb = jnp.dot(ob_ref[...], wb_ref[...], preferred_element_type=F32)
    u = _sigmoid(ga_ref[...].astype(F32)) * ya + _sigmoid(gb_ref[...].astype(F32)) * yb
    out_ref[...] = u.astype(out_ref.dtype)


def merge_projection(o_a, o_b, w_a, w_b, proj, col_ga, col_gb):
    M, K = o_a.shape
    N = w_a.shape[1]
    bm, bn = min(1024, M), min(1024, N)
    return pl.pallas_call(
        _merge_proj_kernel,
        out_shape=jax.ShapeDtypeStruct((M, N), BF16),
        grid=(M // bm, N // bn),
        in_specs=[pl.BlockSpec((bm, K), lambda m, n: (m, 0)),
                  pl.BlockSpec((bm, K), lambda m, n: (m, 0)),
                  pl.BlockSpec((K, bn), lambda m, n: (0, n)),
                  pl.BlockSpec((K, bn), lambda m, n: (0, n)),
                  pl.BlockSpec((bm, bn), lambda m, n: (m, col_ga // bn + n)),
                  pl.BlockSpec((bm, bn), lambda m, n: (m, col_gb // bn + n))],
        out_specs=pl.BlockSpec((bm, bn), lambda m, n: (m, n)),
        compiler_params=_params("parallel", "parallel"),
        name="merge_projection",
    )(o_a, o_b, w_a, w_b, proj, proj)


def _proj_residual_kernel(u_ref, w_ref, x_ref, gate_ref, out_ref):
    y = jnp.dot(u_ref[...], w_ref[...], preferred_element_type=F32)
    out_ref[...] = x_ref[...] + gate_ref[...] * y


def projection_residual(u, w, x, mod_rows, layer, which_gate):
    nb, S, K = u.shape
    D = w.shape[1]
    bm, bn = min(512, S), min(1024, D)
    return pl.pallas_call(
        _proj_residual_kernel,
        out_shape=jax.ShapeDtypeStruct((nb, S, D), F32),
        grid=(nb, S // bm, D // bn),
        in_specs=[pl.BlockSpec((None, bm, K), lambda b, m, n: (b, m, 0)),
                  pl.BlockSpec((K, bn), lambda b, m, n: (0, n)),
                  pl.BlockSpec((None, bm, bn), lambda b, m, n: (b, m, n)),
                  pl.BlockSpec((None, 1, bn), lambda b, m, n: ((layer * nb + b) * N_MOD + which_gate, 0, n))],
        out_specs=pl.BlockSpec((None, bm, bn), lambda b, m, n: (b, m, n)),
        compiler_params=_params("parallel", "parallel", "parallel"),
        name="projection_residual",
    )(u, w, x, mod_rows)


def _norm_router_kernel(x_ref, w_ref, shift_ref, scale_ref, wr_hi_ref, wr_lo_ref, h_ref, logit_ref):
    h = _norm_mod(x_ref[...], w_ref[...], shift_ref[...], scale_ref[...])
    h_hi, h_lo = _split_bf16(h)
    h_ref[...] = h_hi
    nt = (((1,), (1,)), ((), ()))
    logit_ref[...] = (lax.dot_general(wr_hi_ref[...], h_hi, nt, preferred_element_type=F32)
                      + lax.dot_general(wr_hi_ref[...], h_lo, nt, preferred_element_type=F32)
                      + lax.dot_general(wr_lo_ref[...], h_hi, nt, preferred_element_type=F32))


def norm_router(x, norm_w, mod_rows, layer, w_router):
    nb, S, D = x.shape
    E = w_router.shape[1]
    ts = min(512, S)
    wr_hi, wr_lo = _split_bf16(w_router.T)
    return pl.pallas_call(
        _norm_router_kernel,
        out_shape=(jax.ShapeDtypeStruct((nb, S, D), BF16), jax.ShapeDtypeStruct((E, nb * S), F32)),
        grid=(nb, S // ts),
        in_specs=[pl.BlockSpec((None, ts, D), lambda b, s: (b, s, 0)),
                  pl.BlockSpec((None, 1, D), lambda b, s: (layer, 0, 0)),
                  _mod_spec(D, 3, layer, nb),
                  _mod_spec(D, 4, layer, nb),
                  pl.BlockSpec((E, D), lambda b, s: (0, 0)),
                  pl.BlockSpec((E, D), lambda b, s: (0, 0))],
        out_specs=(pl.BlockSpec((None, ts, D), lambda b, s: (b, s, 0)),
                   pl.BlockSpec((E, ts), lambda b, s: (0, b * (S // ts) + s))),
        compiler_params=_params("parallel", "parallel"),
        name="norm_router",
    )(x, norm_w, mod_rows, mod_rows, wr_hi, wr_lo)


def _rank_lt(vals, n_rows, limit):
    ridx = lax.broadcasted_iota(jnp.int32, vals.shape, 0)
    cnt = jnp.zeros(vals.shape, jnp.int32)
    for r in range(n_rows):
        other = vals[r:r + 1, :]
        beats = (other > vals) | ((other == vals) & (r < ridx))
        cnt = cnt + beats.astype(jnp.int32)
    return cnt < limit


def _route_kernel(logit_ref, bias_ref, combine_ref):
    scores = _sigmoid(logit_ref[...])
    sel = scores + bias_ref[...]
    T = sel.shape[1]
    sub = lax.broadcasted_iota(jnp.int32, (GROUP_SIZE, T), 0)
    gscores = []
    for g in range(N_GROUPS):
        v = sel[g * GROUP_SIZE:(g + 1) * GROUP_SIZE, :]
        m1 = jnp.max(v, axis=0, keepdims=True)
        first = jnp.min(jnp.where(v == m1, sub, GROUP_SIZE), axis=0, keepdims=True)
        m2 = jnp.max(jnp.where(sub == first, -jnp.inf, v), axis=0, keepdims=True)
        gscores.append(m1 + m2)
    gsc = jnp.concatenate(gscores, axis=0)
    gkeep = _rank_lt(gsc, N_GROUPS, TOPK_GROUPS)
    ekeep = jnp.concatenate(
        [jnp.broadcast_to(gkeep[g:g + 1, :], (GROUP_SIZE, T)) for g in range(N_GROUPS)], axis=0)
    masked = jnp.where(ekeep, sel, -jnp.inf)
    chosen = _rank_lt(masked, N_EXPERTS, TOP_K)
    w = jnp.where(chosen, scores, 0.0)
    combine_ref[...] = w / jnp.sum(w, axis=0, keepdims=True) * ROUTED_SCALE


def route(logits, router_bias):
    E, T = logits.shape
    tt = min(512, T)
    return pl.pallas_call(
        _route_kernel,
        out_shape=jax.ShapeDtypeStruct((E, T), F32),
        grid=(T // tt,),
        in_specs=[pl.BlockSpec((E, tt), lambda t: (0, t)),
                  pl.BlockSpec((E, 1), lambda t: (0, 0))],
        out_specs=pl.BlockSpec((E, tt), lambda t: (0, t)),
        compiler_params=_params("parallel"),
        name="route",
    )(logits, router_bias.reshape(E, 1))


def _experts_kernel(h_ref, comb_ref, wg_ref, wu_ref, wd_ref, out_ref):
    e = pl.program_id(1)

    @pl.when(e == 0)
    def _():
        out_ref[...] = jnp.zeros_like(out_ref)

    h = h_ref[...]
    gate = jnp.dot(h, wg_ref[...], preferred_element_type=F32)
    up = jnp.dot(h, wu_ref[...], preferred_element_type=F32)
    comb = comb_ref[...]
    lane = lax.broadcasted_iota(jnp.int32, comb.shape, 1)
    ce = jnp.sum(jnp.where(lane == e, comb, 0.0), axis=1, keepdims=True)
    act = (_silu(gate) * up * ce).astype(BF16)
    out_ref[...] += jnp.dot(act, wd_ref[...], preferred_element_type=F32)


def routed_experts(h, combine_t, w_gate, w_up, w_down):
    T, D = h.shape
    E, _, F = w_gate.shape
    bm = min(512, T)
    return pl.pallas_call(
        _experts_kernel,
        out_shape=jax.ShapeDtypeStruct((T, D), F32),
        grid=(T // bm, E),
        in_specs=[pl.BlockSpec((bm, D), lambda m, e: (m, 0)),
                  pl.BlockSpec((bm, E), lambda m, e: (m, 0)),
                  pl.BlockSpec((None, D, F), lambda m, e: (e, 0, 0)),
                  pl.BlockSpec((None, D, F), lambda m, e: (e, 0, 0)),
                  pl.BlockSpec((None, F, D), lambda m, e: (e, 0, 0))],
        out_specs=pl.BlockSpec((bm, D), lambda m, e: (m, 0)),
        compiler_params=_params("parallel", "arbitrary"),
        name="routed_experts",
    )(h, combine_t, w_gate, w_up, w_down)


def _shared_residual_kernel(h_ref, wg_ref, wu_ref, wd_ref, routed_ref, x_ref, gate_ref, out_ref):
    h = h_ref[...]
    act = (_silu(jnp.dot(h, wg_ref[...], preferred_element_type=F32))
           * jnp.dot(h, wu_ref[...], preferred_element_type=F32)).astype(BF16)
    y = routed_ref[...] + jnp.dot(act, wd_ref[...], preferred_element_type=F32)
    out_ref[...] = x_ref[...] + gate_ref[...] * y


def shared_expert_residual(h, w_gate, w_up, w_down, routed, x, mod_rows, layer, which_gate):
    nb, S, D = x.shape
    F = w_gate.shape[1]
    bm = min(256, S)
    row = pl.BlockSpec((None, bm, D), lambda b, m: (b, m, 0))
    return pl.pallas_call(
        _shared_residual_kernel,
        out_shape=jax.ShapeDtypeStruct((nb, S, D), F32),
        grid=(nb, S // bm),
        in_specs=[row,
                  pl.BlockSpec((D, F), lambda b, m: (0, 0)),
                  pl.BlockSpec((D, F), lambda b, m: (0, 0)),
                  pl.BlockSpec((F, D), lambda b, m: (0, 0)),
                  row, row,
                  pl.BlockSpec((None, 1, D), lambda b, m: ((layer * nb + b) * N_MOD + which_gate, 0, 0))],
        out_specs=row,
        compiler_params=_params("parallel", "parallel"),
        name="shared_expert_residual",
    )(h, w_gate, w_up, w_down, routed, x, mod_rows)


def _rmsnorm_kernel(x_ref, w_ref, out_ref):
    x = x_ref[...]
    out_ref[...] = x * lax.rsqrt(jnp.mean(x * x, axis=-1, keepdims=True) + EPS) * w_ref[...]


def rmsnorm(x, w):
    nb, S, D = x.shape
    ts = min(512, S)
    return pl.pallas_call(
        _rmsnorm_kernel,
        out_shape=jax.ShapeDtypeStruct((nb, S, D), F32),
        grid=(nb, S // ts),
        in_specs=[pl.BlockSpec((None, ts, D), lambda b, s: (b, s, 0)),
                  pl.BlockSpec((1, D), lambda b, s: (0, 0))],
        out_specs=pl.BlockSpec((None, ts, D), lambda b, s: (b, s, 0)),
        compiler_params=_params("parallel", "parallel"),
        name="final_rmsnorm",
    )(x, w.reshape(1, D))


def kernel(x, c, norm_mix, norm_ffn, w_ada, b_ada, w_in, hgrn_lower_bounds, hgrn_out_norm, w_proj_a, w_proj_b, w_out, w_router, router_bias, w_exp_gate, w_exp_up, w_exp_down, w_sh_gate, w_sh_up, w_sh_down, norm_final):
    nb, S, D = x.shape
    depth = w_in.shape[0]
    a_width = w_proj_a.shape[1]
    b_width = w_proj_b.shape[1]
    in_cols = w_in.shape[2]
    col_va = 2 * a_width
    col_qb = 3 * a_width
    col_fb = col_qb + b_width
    col_ib = col_fb + b_width
    col_gb = col_ib + b_width
    col_gate_a = col_gb + b_width
    col_gate_b = col_gate_a + D

    lb_sm = jax.nn.softmax(hgrn_lower_bounds.astype(F32), axis=0)
    lb_all = jnp.cumsum(lb_sm, axis=0) - lb_sm[0:1]
    cos, sin = rope_tables(S)

    mod = adaln_modulation(c, w_ada, b_ada)
    mod_rows = mod.reshape(depth * nb * N_MOD, 1, D)

    for l in range(depth):
        h = norm_modulate(x, norm_mix[:, None, :], mod_rows, l, 0, 1)
        proj = matmul(h.reshape(nb * S, D), w_in[l].astype(BF16), BF16).reshape(nb, S, in_cols)

        qk = rope_qk(proj, 2 * a_width, cos, sin)
        outs, lses = [], []
        for window, dilation in DILATED_PATTERNS:
            assert window // dilation == SPAN
            o_g, lse_g = dilated_attention_branch(qk, proj, col_va, a_width, dilation)
            outs.append(o_g)
            lses.append(lse_g)
        o_a = attention_merge(outs, lses)

        o_b = hgrn2(proj, col_qb, col_fb, col_ib, col_gb, b_width,
                    lb_all[l][None, :], hgrn_out_norm[l][None, :])

        u = merge_projection(o_a.reshape(nb * S, a_width), o_b.reshape(nb * S, b_width),
                             w_proj_a[l].astype(BF16), w_proj_b[l].astype(BF16),
                             proj.reshape(nb * S, in_cols), col_gate_a, col_gate_b)
        x = projection_residual(u.reshape(nb, S, D), w_out[l].astype(BF16), x, mod_rows, l, 2)

        h2, logits = norm_router(x, norm_ffn[:, None, :], mod_rows, l, w_router[l])
        combine = route(logits, router_bias[l])
        routed = routed_experts(h2.reshape(nb * S, D), combine.T,
                                w_exp_gate[l].astype(BF16), w_exp_up[l].astype(BF16),
                                w_exp_down[l].astype(BF16))
        x = shared_expert_residual(h2, w_sh_gate[l].astype(BF16), w_sh_up[l].astype(BF16),
                                   w_sh_down[l].astype(BF16), routed.reshape(nb, S, D), x,
                                   mod_rows, l, 5)
    return rmsnorm(x, norm_final)
```

```python
import functools

import jax
import jax.numpy as jnp
import numpy as np
from jax import lax
from jax.experimental import pallas as pl
from jax.experimental.pallas import tpu as pltpu

HEAD_DIM = 128
DILATED_PATTERNS = ((128, 1), (512, 4), (2048, 16))
SPAN = 128
ROPE_THETA = 10000.0
N_EXPERTS = 64
TOP_K = 8
N_GROUPS = 8
TOPK_GROUPS = 4
GROUP_SIZE = N_EXPERTS // N_GROUPS
ROUTED_SCALE = 2.5
N_MOD = 6
EPS = 1e-6

LANES = 128
SUBLANES = 8
VMEM_LIMIT_BYTES = 56 * 1024 * 1024

HGRN_CHUNK = 128
HGRN_LEVELS = 7

EXPERT_TILE = 256
DMA_CHUNK = 32

BF16 = jnp.bfloat16
F32 = jnp.float32
U32 = jnp.uint32
I32 = jnp.int32

_NT = (((1,), (1,)), ((), ()))
_TN = (((0,), (0,)), ((), ()))


def _params(*sem):
    return pltpu.CompilerParams(dimension_semantics=sem, vmem_limit_bytes=VMEM_LIMIT_BYTES)


def _sigmoid(x):
    return 1.0 / (1.0 + jnp.exp(-x))


def _silu(x):
    return x * _sigmoid(x)


def _split_bf16(x):
    hi = x.astype(BF16)
    lo = (x - hi.astype(F32)).astype(BF16)
    return hi, lo


def _bf16_bits(x):
    b = lax.bitcast_convert_type(x, U32)
    b = b + jnp.uint32(0x7FFF) + ((b >> 16) & jnp.uint32(1))
    return b & jnp.uint32(0xFFFF0000)


def _pack_pair(lo, hi):
    return (_bf16_bits(lo) >> 16) | _bf16_bits(hi)


def _unpack_pair(p):
    lo = lax.bitcast_convert_type(p << 16, F32)
    hi = lax.bitcast_convert_type(p & jnp.uint32(0xFFFF0000), F32)
    return lo, hi


def _adaln_kernel(cb_ref, w_ref, bias_ref, out_ref, cs_ref, *, kc):
    K, bn = w_ref.shape
    nb = cb_ref.shape[0]

    @pl.when((pl.program_id(0) == 0) & (pl.program_id(1) == 0))
    def _():
        cs_ref[...] = _silu(cb_ref[...])

    for j in range(bn // LANES):
        cols = slice(j * LANES, (j + 1) * LANES)

        def body(i, accs):
            k0 = pl.multiple_of(i * kc, kc)
            w = w_ref[pl.ds(k0, kc), cols]
            out = []
            for b in range(nb):
                p = (w * cs_ref[b, pl.ds(k0, kc), :]).reshape(kc // SUBLANES, SUBLANES, LANES)
                out.append(accs[b] + jnp.sum(p, axis=0))
            return tuple(out)

        accs = lax.fori_loop(0, K // kc, body,
                             tuple(jnp.zeros((SUBLANES, LANES), F32) for _ in range(nb)))
        for b in range(nb):
            out_ref[b:b + 1, cols] = jnp.sum(accs[b], axis=0, keepdims=True) + bias_ref[:, cols]


def adaln_modulation(c, w_ada, b_ada):
    nl, K, N = w_ada.shape
    nb = c.shape[0]
    bn = min(512, N)
    kc = min(64, K)
    cb = jnp.broadcast_to(c[:, :, None], (nb, K, LANES))
    return pl.pallas_call(
        functools.partial(_adaln_kernel, kc=kc),
        out_shape=jax.ShapeDtypeStruct((nl, nb, N), F32),
        grid=(nl, N // bn),
        in_specs=[
            pl.BlockSpec((nb, K, LANES), lambda l, n: (0, 0, 0)),
            pl.BlockSpec((None, K, bn), lambda l, n: (l, 0, n)),
            pl.BlockSpec((None, 1, bn), lambda l, n: (l, 0, n)),
        ],
        out_specs=pl.BlockSpec((None, nb, bn), lambda l, n: (l, 0, n)),
        scratch_shapes=[pltpu.VMEM((nb, K, LANES), F32)],
        compiler_params=_params("arbitrary", "arbitrary"),
        name="adaln_modulation",
    )(cb, w_ada, b_ada.reshape(nl, 1, N))


def _norm_mod(x, w, shift, scale):
    y = x * lax.rsqrt(jnp.mean(x * x, axis=-1, keepdims=True) + EPS) * w
    return y * (1.0 + scale) + shift


def _norm_mod_kernel(x_ref, w_ref, shift_ref, scale_ref, out_ref):
    out_ref[...] = _norm_mod(x_ref[...], w_ref[...], shift_ref[...], scale_ref[...]).astype(out_ref.dtype)


def _mod_spec(D, which, layer, nb):
    return pl.BlockSpec((None, 1, D), lambda b, *_: ((layer * nb + b) * N_MOD + which, 0, 0))


def norm_modulate(x, norm_w, mod_rows, layer, which_shift, which_scale):
    nb, S, D = x.shape
    ts = min(512, S)
    return pl.pallas_call(
        _norm_mod_kernel,
        out_shape=jax.ShapeDtypeStruct((nb, S, D), BF16),
        grid=(nb, S // ts),
        in_specs=[
            pl.BlockSpec((None, ts, D), lambda b, s: (b, s, 0)),
            pl.BlockSpec((None, 1, D), lambda b, s: (layer, 0, 0)),
            _mod_spec(D, which_shift, layer, nb),
            _mod_spec(D, which_scale, layer, nb),
        ],
        out_specs=pl.BlockSpec((None, ts, D), lambda b, s: (b, s, 0)),
        compiler_params=_params("parallel", "parallel"),
        name="norm_modulate",
    )(x, norm_w, mod_rows, mod_rows)


def _qkv_proj_kernel(x_ref, w_ref, cos_ref, sin_ref, *refs, n_rope_tiles, dilations):
    out_refs, scr = refs[:-1], refs[-1]
    n_heads, bm, _ = scr.shape
    y = jnp.dot(x_ref[...], w_ref[...].astype(BF16), preferred_element_type=F32)
    n = pl.program_id(2)

    @pl.when(n < n_rope_tiles)
    def _():
        cos = cos_ref[...]
        sin = sin_ref[...]
        for h in range(n_heads):
            t = y[:, h * HEAD_DIM:(h + 1) * HEAD_DIM]
            scr[h] = t * cos + pltpu.roll(t, HEAD_DIM // 2, 1) * sin

    @pl.when(n >= n_rope_tiles)
    def _():
        for h in range(n_heads):
            scr[h] = y[:, h * HEAD_DIM:(h + 1) * HEAD_DIM]

    for d, o_ref in zip(dilations, out_refs):
        for r in range(d):
            for h in range(n_heads):
                o_ref[r, :, h * HEAD_DIM:(h + 1) * HEAD_DIM] = (
                    scr[h, pl.ds(r, bm // d, stride=d), :].astype(o_ref.dtype))


def rope_tables(S):
    half = HEAD_DIM // 2
    inv = ROPE_THETA ** (-jnp.arange(half, dtype=F32) / half)
    ang = jnp.arange(S, dtype=F32)[:, None] * inv[None, :]
    cos, sin = jnp.cos(ang), jnp.sin(ang)
    return jnp.concatenate([cos, cos], axis=-1), jnp.concatenate([-sin, sin], axis=-1)


def qkv_projection(h, w_in, layer, qkv_cols, rope_cols, cos, sin, dilations):
    nb, S, D = h.shape
    bm, bn = min(1024, S), min(512, qkv_cols)
    outs = tuple(jax.ShapeDtypeStruct((nb, d, S // d, qkv_cols), BF16) for d in dilations)
    out_specs = tuple(pl.BlockSpec((None, d, bm // d, bn), lambda b, m, n: (b, 0, m, n)) for d in dilations)
    return pl.pallas_call(
        functools.partial(_qkv_proj_kernel, n_rope_tiles=rope_cols // bn, dilations=dilations),
        out_shape=outs,
        grid=(nb, S // bm, qkv_cols // bn),
        in_specs=[pl.BlockSpec((None, bm, D), lambda b, m, n: (b, m, 0)),
                  pl.BlockSpec((None, D, bn), lambda b, m, n: (layer, 0, n)),
                  pl.BlockSpec((bm, HEAD_DIM), lambda b, m, n: (m, 0)),
                  pl.BlockSpec((bm, HEAD_DIM), lambda b, m, n: (m, 0))],
        out_specs=out_specs,
        scratch_shapes=[pltpu.VMEM((bn // HEAD_DIM, bm, HEAD_DIM), F32)],
        compiler_params=_params("parallel", "parallel", "arbitrary"),
        name="qkv_projection",
    )(h, w_in, cos, sin)


def _matmul_kernel(x_ref, w_ref, out_ref):
    out_ref[...] = jnp.dot(x_ref[...], w_ref[...].astype(BF16),
                           preferred_element_type=F32).astype(out_ref.dtype)


def matmul_cols(x, w, layer, col0, ncols, out_dtype):
    M, K = x.shape
    bm, bn = min(1024, M), min(512, ncols)
    return pl.pallas_call(
        _matmul_kernel,
        out_shape=jax.ShapeDtypeStruct((M, ncols), out_dtype),
        grid=(M // bm, ncols // bn),
        in_specs=[pl.BlockSpec((bm, K), lambda m, n: (m, 0)),
                  pl.BlockSpec((None, K, bn), lambda m, n: (layer, 0, col0 // bn + n))],
        out_specs=pl.BlockSpec((bm, bn), lambda m, n: (m, n)),
        compiler_params=_params("parallel", "arbitrary"),
        name="matmul_cols",
    )(x, w)


def _attn_kernel(q_ref, kp_ref, kc_ref, vp_ref, vc_ref, o_ref, lse_ref):
    i = pl.program_id(2)
    n_heads = q_ref.shape[-1] // HEAD_DIM
    qi = lax.broadcasted_iota(I32, (SPAN, 2 * SPAN), 0)
    kj = lax.broadcasted_iota(I32, (SPAN, 2 * SPAN), 1)
    mask = ((kj < SPAN) & (kj >= qi) & (i > 0)) | ((kj >= SPAN) & ((kj - SPAN) <= qi))
    scale = HEAD_DIM ** -0.5
    s = []
    for h in range(n_heads):
        cols = slice(h * HEAD_DIM, (h + 1) * HEAD_DIM)
        k_h = jnp.concatenate([kp_ref[:, cols], kc_ref[:, cols]], axis=0)
        s.append(lax.dot_general(q_ref[:, cols], k_h, _NT, preferred_element_type=F32))
    s = jnp.stack(s, axis=0)
    s = jnp.where(mask[None], s * scale, -jnp.inf)
    m = jnp.max(s, axis=-1, keepdims=True)
    p = jnp.exp(s - m).astype(BF16)
    ones = jnp.ones((2 * SPAN, HEAD_DIM), BF16)
    lses = []
    for h in range(n_heads):
        cols = slice(h * HEAD_DIM, (h + 1) * HEAD_DIM)
        v_h = jnp.concatenate([vp_ref[:, cols], vc_ref[:, cols]], axis=0)
        oe = jnp.dot(p[h], jnp.concatenate([v_h, ones], axis=1), preferred_element_type=F32)
        den = oe[:, HEAD_DIM:]
        o_ref[:, cols] = (oe[:, :HEAD_DIM] / den).astype(o_ref.dtype)
        lses.append(m[h] + jnp.log(den[:, :1]))
    lse_ref[...] = jnp.concatenate(lses, axis=-1)


def dilated_attention_branch(qkv, width):
    nb, d, L, _ = qkv.shape
    nblk = L // SPAN
    n_heads = width // HEAD_DIM

    def spec(col, prev):
        if prev:
            return pl.BlockSpec((None, None, SPAN, width), lambda b, r, i: (b, r, jnp.maximum(i - 1, 0), col))
        return pl.BlockSpec((None, None, SPAN, width), lambda b, r, i: (b, r, i, col))

    return pl.pallas_call(
        _attn_kernel,
        out_shape=(jax.ShapeDtypeStruct((nb, d, L, width), BF16),
                   jax.ShapeDtypeStruct((nb, d, L, n_heads), F32)),
        grid=(nb, d, nblk),
        in_specs=[spec(0, False), spec(1, True), spec(1, False), spec(2, True), spec(2, False)],
        out_specs=(pl.BlockSpec((None, None, SPAN, width), lambda b, r, i: (b, r, i, 0)),
                   pl.BlockSpec((None, None, SPAN, n_heads), lambda b, r, i: (b, r, i, 0))),
        compiler_params=_params("parallel", "parallel", "arbitrary"),
        name=f"dilated_attention_d{d}",
    )(qkv, qkv, qkv, qkv, qkv)


def _attn_merge_kernel(*refs, dilations):
    n = len(dilations)
    o_refs, lse_refs, out_ref, scr_refs = refs[:n], refs[n:2 * n], refs[2 * n], refs[2 * n + 1:]
    ts = out_ref.shape[0]
    nat = []
    for g, d in enumerate(dilations):
        if d == 1:
            nat.append(None)
            continue
        scr = scr_refs[len([x for x in nat if x is not None])]
        for r in range(d):
            for h in range(scr.shape[0]):
                scr[h, pl.ds(r, ts // d, stride=d), :] = (
                    o_refs[g][r, :, h * HEAD_DIM:(h + 1) * HEAD_DIM].astype(F32))
        nat.append(scr)
    lses = [r[...] for r in lse_refs]
    m = functools.reduce(jnp.maximum, lses)
    es = [jnp.exp(l - m) for l in lses]
    tot = functools.reduce(jnp.add, es)
    ws = [e / tot for e in es]
    for h in range(out_ref.shape[-1] // HEAD_DIM):
        cols = slice(h * HEAD_DIM, (h + 1) * HEAD_DIM)
        acc = jnp.zeros((ts, HEAD_DIM), F32)
        for g in range(n):
            o = o_refs[g][0, :, cols].astype(F32) if nat[g] is None else nat[g][h]
            acc = acc + ws[g][:, h:h + 1] * o
        out_ref[:, cols] = acc.astype(out_ref.dtype)


def attention_merge(outs, lses, dilations):
    nb, _, S, width = outs[dilations.index(1)].shape
    n_heads = lses[0].shape[-1]
    ts = min(512, S)
    o_specs = [pl.BlockSpec((None, d, ts // d, width), lambda b, s: (b, 0, s, 0)) for d in dilations]
    l_spec = pl.BlockSpec((None, ts, n_heads), lambda b, s: (b, s, 0))
    return pl.pallas_call(
        functools.partial(_attn_merge_kernel, dilations=dilations),
        out_shape=jax.ShapeDtypeStruct((nb, S, width), BF16),
        grid=(nb, S // ts),
        in_specs=o_specs + [l_spec] * len(lses),
        out_specs=pl.BlockSpec((None, ts, width), lambda b, s: (b, s, 0)),
        scratch_shapes=[pltpu.VMEM((n_heads, ts, HEAD_DIM), F32) for d in dilations if d > 1],
        compiler_params=_params("parallel", "parallel"),
        name="attention_merge",
    )(*outs, *lses)


def _hgrn_sum_matrix(C, levels):
    t = np.arange(C)[:, None]
    u = np.arange(C)[None, :]
    blocks = [(u <= t), (u > t)]
    for j in range(levels):
        half = C >> (j + 1)
        mid = (t // (2 * half)) * (2 * half) + half - 1
        upper = (t // half) % 2 == 1
        blocks.append(np.where(upper, (u > mid) & (u <= t), (u > t) & (u <= mid)))
    return np.concatenate(blocks, axis=0).astype(np.float32)


def _hgrn_kernel(q_ref, f_ref, i_ref, g_ref, lb_ref, nw_ref, sm_ref, out_ref, state_ref, *, n_chunks):
    C = HGRN_CHUNK

    @pl.when(pl.program_id(2) == 0)
    def _():
        state_ref[...] = jnp.zeros_like(state_ref)

    ti = lax.broadcasted_iota(I32, (C, C), 0)
    si = lax.broadcasted_iota(I32, (C, C), 1)
    xor = ti ^ si
    lower = si < ti
    lb = lb_ref[...]
    nw = nw_ref[...]
    ones = jnp.ones((C, LANES), BF16)

    for c in range(n_chunks):
        rows = slice(c * C, (c + 1) * C)
        q = _silu(q_ref[rows, :].astype(F32))
        f = lb + (1.0 - lb) * _sigmoid(f_ref[rows, :].astype(F32))
        k = 1.0 - f
        g = jnp.log(f)
        v = i_ref[rows, :]
        g_hi, g_lo = _split_bf16(g)
        ghl = jnp.concatenate([g_hi, g_lo], axis=-1)
        e2 = jnp.dot(sm_ref[...], ghl, preferred_element_type=F32)
        e = e2[:, :HEAD_DIM] + e2[:, HEAD_DIM:]
        b_end = (lax.dot_general(g_hi, ones, _TN, preferred_element_type=F32)
                 + lax.dot_general(g_lo, ones, _TN, preferred_element_type=F32))

        scores = jnp.where(ti == si, lax.dot_general(q.astype(BF16), k.astype(BF16), _NT,
                                                     preferred_element_type=F32), 0.0)
        for j in range(HGRN_LEVELS):
            a = jnp.exp(e[(2 + j) * C:(3 + j) * C, :])
            s_j = lax.dot_general((q * a).astype(BF16), (k * a).astype(BF16), _NT,
                                  preferred_element_type=F32)
            scores = scores + jnp.where(lower & ((xor >> (HGRN_LEVELS - 1 - j)) == 1), s_j, 0.0)

        state = state_ref[...]
        o = jnp.dot(scores.astype(BF16), v, preferred_element_type=F32)
        o = o + jnp.dot((q * jnp.exp(e[0:C, :])).astype(BF16), state.astype(BF16),
                        preferred_element_type=F32)
        k_end = (k * jnp.exp(e[C:2 * C, :])).astype(BF16)
        state_ref[...] = jnp.exp(b_end) * state + lax.dot_general(k_end, v, _TN, preferred_element_type=F32)

        y = o * lax.rsqrt(jnp.mean(o * o, axis=-1, keepdims=True) + EPS) * nw
        out_ref[rows, :] = (y * _silu(g_ref[rows, :].astype(F32))).astype(out_ref.dtype)


def hgrn2(proj, col_q, col_f, col_i, col_g, width, lb, out_norm):
    nb, S, _ = proj.shape
    n_heads = width // HEAD_DIM
    C = HGRN_CHUNK
    ts = min(512, S)
    sm = jnp.asarray(_hgrn_sum_matrix(C, HGRN_LEVELS), dtype=BF16)

    def col_spec(col0):
        blk = col0 // HEAD_DIM
        return pl.BlockSpec((None, ts, HEAD_DIM), lambda b, h, s: (b, s, blk + h))

    return pl.pallas_call(
        functools.partial(_hgrn_kernel, n_chunks=ts // C),
        out_shape=jax.ShapeDtypeStruct((nb, S, width), BF16),
        grid=(nb, n_heads, S // ts),
        in_specs=[col_spec(col_q), col_spec(col_f), col_spec(col_i), col_spec(col_g),
                  pl.BlockSpec((1, HEAD_DIM), lambda b, h, s: (0, h)),
                  pl.BlockSpec((1, HEAD_DIM), lambda b, h, s: (0, 0)),
                  pl.BlockSpec(sm.shape, lambda b, h, s: (0, 0))],
        out_specs=pl.BlockSpec((None, ts, HEAD_DIM), lambda b, h, s: (b, s, h)),
        scratch_shapes=[pltpu.VMEM((HEAD_DIM, HEAD_DIM), F32)],
        compiler_params=_params("parallel", "parallel", "arbitrary"),
        name="hgrn2",
    )(proj, proj, proj, proj, lb, out_norm, sm)


def _merge_proj_kernel(oa_ref, ob_ref, wa_ref, wb_ref, ga_ref, gb_ref, out_ref):
    ya = jnp.dot(oa_ref[...], wa_ref[...].astype(BF16), preferred_element_type=F32)
    yb = jnp.dot(ob_ref[...], wb_ref[...].astype(BF16), preferred_element_type=F32)
    u = _sigmoid(ga_ref[...].astype(F32)) * ya + _sigmoid(gb_ref[...].astype(F32)) * yb
    out_ref[...] = u.astype(out_ref.dtype)


def merge_projection(o_a, o_b, w_a, w_b, layer, proj, col_ga, col_gb):
    M, K = o_a.shape
    N = w_a.shape[2]
    bm, bn = min(1024, M), min(512, N)
    return pl.pallas_call(
        _merge_proj_kernel,
        out_shape=jax.ShapeDtypeStruct((M, N), BF16),
        grid=(M // bm, N // bn),
        in_specs=[pl.BlockSpec((bm, K), lambda m, n: (m, 0)),
                  pl.BlockSpec((bm, K), lambda m, n: (m, 0)),
                  pl.BlockSpec((None, K, bn), lambda m, n: (layer, 0, n)),
                  pl.BlockSpec((None, K, bn), lambda m, n: (layer, 0, n)),
                  pl.BlockSpec((bm, bn), lambda m, n: (m, col_ga // bn + n)),
                  pl.BlockSpec((bm, bn), lambda m, n: (m, col_gb // bn + n))],
        out_specs=pl.BlockSpec((bm, bn), lambda m, n: (m, n)),
        compiler_params=_params("parallel", "arbitrary"),
        name="merge_projection",
    )(o_a, o_b, w_a, w_b, proj, proj)


def _proj_residual_kernel(u_ref, w_ref, x_ref, gate_ref, out_ref):
    y = jnp.dot(u_ref[...], w_ref[...].astype(BF16), preferred_element_type=F32)
    out_ref[...] = x_ref[...] + gate_ref[...] * y


def projection_residual(u, w, x, mod_rows, layer, which_gate):
    nb, S, K = u.shape
    D = w.shape[2]
    bm, bn = min(1024, S), min(512, D)
    return pl.pallas_call(
        _proj_residual_kernel,
        out_shape=jax.ShapeDtypeStruct((nb, S, D), F32),
        grid=(nb, S // bm, D // bn),
        in_specs=[pl.BlockSpec((None, bm, K), lambda b, m, n: (b, m, 0)),
                  pl.BlockSpec((None, K, bn), lambda b, m, n: (layer, 0, n)),
                  pl.BlockSpec((None, bm, bn), lambda b, m, n: (b, m, n)),
                  pl.BlockSpec((None, 1, bn), lambda b, m, n: ((layer * nb + b) * N_MOD + which_gate, 0, n))],
        out_specs=pl.BlockSpec((None, bm, bn), lambda b, m, n: (b, m, n)),
        compiler_params=_params("parallel", "parallel", "arbitrary"),
        name="projection_residual",
    )(u, w, x, mod_rows)


def _norm_router_kernel(x_ref, w_ref, shift_ref, scale_ref, wr_hi_ref, wr_lo_ref, hp_ref, logit_ref):
    h = _norm_mod(x_ref[...], w_ref[...], shift_ref[...], scale_ref[...])
    half = h.shape[1] // 2
    bits = _bf16_bits(h)
    hp_ref[...] = (bits[:, :half] >> 16) | bits[:, half:]
    h_r = lax.bitcast_convert_type(bits, F32)
    h_hi = h_r.astype(BF16)
    h_lo = (h - h_r).astype(BF16)
    logit_ref[...] = (lax.dot_general(wr_hi_ref[...], h_hi, _NT, preferred_element_type=F32)
                      + lax.dot_general(wr_hi_ref[...], h_lo, _NT, preferred_element_type=F32)
                      + lax.dot_general(wr_lo_ref[...], h_hi, _NT, preferred_element_type=F32))


def norm_router(x, norm_w, mod_rows, layer, w_router):
    nb, S, D = x.shape
    E = w_router.shape[1]
    ts = min(512, S)
    nts = S // ts
    wr_hi, wr_lo = _split_bf16(w_router.T)
    return pl.pallas_call(
        _norm_router_kernel,
        out_shape=(jax.ShapeDtypeStruct((nb * S, D // 2), U32), jax.ShapeDtypeStruct((E, nb * S), F32)),
        grid=(nb, nts),
        in_specs=[pl.BlockSpec((None, ts, D), lambda b, s: (b, s, 0)),
                  pl.BlockSpec((None, 1, D), lambda b, s: (layer, 0, 0)),
                  _mod_spec(D, 3, layer, nb),
                  _mod_spec(D, 4, layer, nb),
                  pl.BlockSpec((E, D), lambda b, s: (0, 0)),
                  pl.BlockSpec((E, D), lambda b, s: (0, 0))],
        out_specs=(pl.BlockSpec((ts, D // 2), lambda b, s: (b * nts + s, 0)),
                   pl.BlockSpec((E, ts), lambda b, s: (0, b * nts + s))),
        compiler_params=_params("parallel", "parallel"),
        name="norm_router",
    )(x, norm_w, mod_rows, mod_rows, wr_hi, wr_lo)


def _rank_lt(vals, n_rows, limit):
    ridx = lax.broadcasted_iota(I32, vals.shape, 0)
    cnt = jnp.zeros(vals.shape, I32)
    for r in range(n_rows):
        other = vals[r:r + 1, :]
        beats = (other > vals) | ((other == vals) & (r < ridx))
        cnt = cnt + beats.astype(I32)
    return cnt < limit


def _route_kernel(logit_ref, bias_ref, trie_ref, trit_ref, eid_ref, pos_ref, wt_ref, cnt_ref, carry_ref):
    @pl.when(pl.program_id(0) == 0)
    def _():
        carry_ref[...] = jnp.zeros_like(carry_ref)

    scores = _sigmoid(logit_ref[...])
    sel = scores + bias_ref[...]
    E, T = sel.shape
    sub = lax.broadcasted_iota(I32, (GROUP_SIZE, T), 0)
    gscores = []
    for g in range(N_GROUPS):
        v = sel[g * GROUP_SIZE:(g + 1) * GROUP_SIZE, :]
        m1 = jnp.max(v, axis=0, keepdims=True)
        first = jnp.min(jnp.where(v == m1, sub, GROUP_SIZE), axis=0, keepdims=True)
        m2 = jnp.max(jnp.where(sub == first, -jnp.inf, v), axis=0, keepdims=True)
        gscores.append(m1 + m2)
    gsc = jnp.concatenate(gscores, axis=0)
    gkeep = _rank_lt(gsc, N_GROUPS, TOPK_GROUPS)
    ekeep = jnp.concatenate(
        [jnp.broadcast_to(gkeep[g:g + 1, :], (GROUP_SIZE, T)) for g in range(N_GROUPS)], axis=0)
    masked = jnp.where(ekeep, sel, -jnp.inf)
    chosen = _rank_lt(masked, N_EXPERTS, TOP_K)
    w = jnp.where(chosen, scores, 0.0)
    combine = w / jnp.sum(w, axis=0, keepdims=True) * ROUTED_SCALE

    cf = jnp.where(chosen, 1.0, 0.0)
    cb = cf.astype(BF16)
    rank = jnp.dot(trie_ref[...], cb, preferred_element_type=F32)
    local = jnp.dot(cb, trit_ref[...], preferred_element_type=F32)
    carry = carry_ref[:, :1]
    pos = carry + local
    eidx = lax.broadcasted_iota(I32, (E, T), 0).astype(F32)
    eids, poss, wts = [], [], []
    for k in range(TOP_K):
        sel_k = chosen & (rank == k)
        eids.append(jnp.sum(jnp.where(sel_k, eidx, 0.0), axis=0, keepdims=True))
        poss.append(jnp.sum(jnp.where(sel_k, pos, 0.0), axis=0, keepdims=True))
        wts.append(jnp.sum(jnp.where(sel_k, combine, 0.0), axis=0, keepdims=True))
    eid_ref[...] = jnp.concatenate(eids, axis=0).astype(I32)
    pos_ref[...] = jnp.concatenate(poss, axis=0).astype(I32)
    wt_ref[...] = jnp.concatenate(wts, axis=0)
    total = carry + jnp.sum(cf, axis=1, keepdims=True)
    carry_ref[...] = jnp.broadcast_to(total, carry_ref.shape)
    cnt_ref[...] = jnp.broadcast_to(total, cnt_ref.shape).astype(I32)


def route(logits, router_bias):
    E, T = logits.shape
    tt = min(512, T)
    trie = jnp.asarray(np.tril(np.ones((E, E), np.float32), -1), BF16)
    trit = jnp.asarray(np.triu(np.ones((tt, tt), np.float32), 1), BF16)
    kt = pl.BlockSpec((TOP_K, tt), lambda t: (0, t))
    return pl.pallas_call(
        _route_kernel,
        out_shape=(jax.ShapeDtypeStruct((TOP_K, T), I32), jax.ShapeDtypeStruct((TOP_K, T), I32),
                   jax.ShapeDtypeStruct((TOP_K, T), F32), jax.ShapeDtypeStruct((E, LANES), I32)),
        grid=(T // tt,),
        in_specs=[pl.BlockSpec((E, tt), lambda t: (0, t)),
                  pl.BlockSpec((E, 1), lambda t: (0, 0)),
                  pl.BlockSpec((E, E), lambda t: (0, 0)),
                  pl.BlockSpec((tt, tt), lambda t: (0, 0))],
        out_specs=(kt, kt, kt, pl.BlockSpec((E, LANES), lambda t: (0, 0))),
        scratch_shapes=[pltpu.VMEM((E, LANES), F32)],
        compiler_params=_params("arbitrary"),
        name="route",
    )(logits, router_bias.reshape(E, 1), trie, trit)


def _dest_kernel(off_ref, eid_ref, pos_ref, dest_ref):
    eid = eid_ref[...]
    acc = pos_ref[...]
    for e in range(N_EXPERTS):
        acc = acc + jnp.where(eid == e, off_ref[e], 0)
    dest_ref[...] = acc


def destination_rows(offsets, eid, pos):
    K, T = eid.shape
    tt = min(2048, T)
    kt = pl.BlockSpec((K, tt), lambda t, off: (0, t))
    return pl.pallas_call(
        _dest_kernel,
        out_shape=jax.ShapeDtypeStruct((K, T), I32),
        grid_spec=pltpu.PrefetchScalarGridSpec(num_scalar_prefetch=1, grid=(T // tt,),
                                               in_specs=[kt, kt], out_specs=kt),
        compiler_params=_params("parallel"),
        name="destination_rows",
    )(offsets, eid, pos)


def _dispatch_kernel(dest_ref, h_hbm, xp_hbm, sem, *, tt):
    t0 = pl.program_id(0) * tt
    n_chunks = tt // DMA_CHUNK

    def row_copy(i, k):
        return pltpu.make_async_copy(h_hbm.at[pl.ds(t0 + i, 1)], xp_hbm.at[pl.ds(dest_ref[k, i], 1)], sem)

    def start_chunk(c):
        def body(j, carry):
            for k in range(TOP_K):
                row_copy(c * DMA_CHUNK + j, k).start()
            return carry
        lax.fori_loop(0, DMA_CHUNK, body, 0)

    def wait_chunk(c):
        def body(j, carry):
            for k in range(TOP_K):
                row_copy(c * DMA_CHUNK + j, k).wait()
            return carry
        lax.fori_loop(0, DMA_CHUNK, body, 0)

    start_chunk(0)

    def step(c, carry):
        start_chunk(c)
        wait_chunk(c - 1)
        return carry

    lax.fori_loop(1, n_chunks, step, 0)
    wait_chunk(n_chunks - 1)


def dispatch_rows(hp, dest):
    T, W = hp.shape
    K = dest.shape[0]
    tt = min(256, T)
    return pl.pallas_call(
        functools.partial(_dispatch_kernel, tt=tt),
        out_shape=jax.ShapeDtypeStruct((K * T, W), U32),
        grid=(T // tt,),
        in_specs=[pl.BlockSpec((K, tt), lambda t: (0, t), memory_space=pltpu.SMEM),
                  pl.BlockSpec(memory_space=pl.ANY)],
        out_specs=pl.BlockSpec(memory_space=pl.ANY),
        scratch_shapes=[pltpu.SemaphoreType.DMA(())],
        compiler_params=pltpu.CompilerParams(dimension_semantics=("arbitrary",), has_side_effects=True),
        name="dispatch_rows",
    )(dest, hp)


def _experts_kernel(tile_ref, eid_ref, lo_ref, hi_ref, first_ref, x_ref, wg_ref, wu_ref, wd_ref, y_ref,
                    wg_s, wu_s, wd_s):
    v = pl.program_id(0)
    prev = eid_ref[jnp.maximum(v - 1, 0)]

    @pl.when((v == 0) | (eid_ref[v] != prev))
    def _():
        wg_s[...] = wg_ref[...].astype(BF16)
        wu_s[...] = wu_ref[...].astype(BF16)
        wd_s[...] = wd_ref[...].astype(BF16)

    lo, hi = lo_ref[v], hi_ref[v]

    @pl.when(hi > lo)
    def _():
        tm, half = x_ref.shape
        x_lo, x_hi = _unpack_pair(x_ref[...])
        x_lo, x_hi = x_lo.astype(BF16), x_hi.astype(BF16)
        gate = (jnp.dot(x_lo, wg_s[:half, :], preferred_element_type=F32)
                + jnp.dot(x_hi, wg_s[half:, :], preferred_element_type=F32))
        up = (jnp.dot(x_lo, wu_s[:half, :], preferred_element_type=F32)
              + jnp.dot(x_hi, wu_s[half:, :], preferred_element_type=F32))
        act = (_silu(gate) * up).astype(BF16)
        packed = _pack_pair(jnp.dot(act, wd_s[:, :half], preferred_element_type=F32),
                            jnp.dot(act, wd_s[:, half:], preferred_element_type=F32))
        row = tile_ref[v] * tm + lax.broadcasted_iota(I32, (tm, 1), 0)
        mine = (row >= lo) & (row < hi)

        @pl.when(first_ref[v] == 1)
        def _():
            y_ref[...] = jnp.where(mine, packed, jnp.uint32(0))

        @pl.when(first_ref[v] == 0)
        def _():
            y_ref[...] = jnp.where(mine, packed, y_ref[...])


def expert_segments(counts, n_rows, tm):
    E = counts.shape[0]
    n_tiles = n_rows // tm
    ends = jnp.cumsum(counts)
    starts = ends - counts
    cuts = jnp.sort(jnp.concatenate([jnp.arange(n_tiles, dtype=I32) * tm, starts.astype(I32)]))
    lo = cuts
    hi = jnp.concatenate([cuts[1:], jnp.array([n_rows], I32)])
    tile = jnp.minimum(lo // tm, n_tiles - 1)
    eid = jnp.minimum(jnp.sum(ends[None, :] <= lo[:, None], axis=1), E - 1).astype(I32)
    first = ((lo % tm == 0) & (hi > lo)).astype(I32)
    return tile.astype(I32), eid, lo.astype(I32), hi.astype(I32), first


def routed_experts(xp, segments, w_gate, w_up, w_down, layer):
    R, half = xp.shape
    _, E, D, F = w_gate.shape
    tm = min(EXPERT_TILE, R)
    n_visits = segments[0].shape[0]
    row = pl.BlockSpec((tm, half), lambda v, tile, eid, lo, hi, first: (tile[v], 0))
    return pl.pallas_call(
        _experts_kernel,
        out_shape=jax.ShapeDtypeStruct((R, half), U32),
        grid_spec=pltpu.PrefetchScalarGridSpec(
            num_scalar_prefetch=5, grid=(n_visits,),
            in_specs=[row,
                      pl.BlockSpec((None, None, D, F), lambda v, tile, eid, lo, hi, first: (layer, eid[v], 0, 0)),
                      pl.BlockSpec((None, None, D, F), lambda v, tile, eid, lo, hi, first: (layer, eid[v], 0, 0)),
                      pl.BlockSpec((None, None, F, D), lambda v, tile, eid, lo, hi, first: (layer, eid[v], 0, 0))],
            out_specs=row,
            scratch_shapes=[pltpu.VMEM((D, F), BF16), pltpu.VMEM((D, F), BF16), pltpu.VMEM((F, D), BF16)]),
        compiler_params=_params("arbitrary"),
        name="routed_experts",
    )(*segments, xp, w_gate, w_up, w_down)


def _combine_kernel(dest_ref, hp_ref, wt_ref, sg_ref, su_ref, sd_ref, x_ref, gate_ref, yp_hbm, out_ref,
                    buf, sg_s, su_s, sd_s, sem, *, tt):
    @pl.when(pl.program_id(0) == 0)
    def _():
        sg_s[...] = sg_ref[...].astype(BF16)
        su_s[...] = su_ref[...].astype(BF16)
        sd_s[...] = sd_ref[...].astype(BF16)

    def row_copy(i, k):
        return pltpu.make_async_copy(yp_hbm.at[pl.ds(dest_ref[k, i], 1)], buf.at[k, pl.ds(i, 1)], sem)

    def start(i, carry):
        for k in range(TOP_K):
            row_copy(i, k).start()
        return carry

    def wait(i, carry):
        for k in range(TOP_K):
            row_copy(i, k).wait()
        return carry

    lax.fori_loop(0, tt, start, 0)

    half = hp_ref.shape[1]
    h_lo, h_hi = _unpack_pair(hp_ref[...])
    h_lo, h_hi = h_lo.astype(BF16), h_hi.astype(BF16)
    g = (jnp.dot(h_lo, sg_s[:half, :], preferred_element_type=F32)
         + jnp.dot(h_hi, sg_s[half:, :], preferred_element_type=F32))
    u = (jnp.dot(h_lo, su_s[:half, :], preferred_element_type=F32)
         + jnp.dot(h_hi, su_s[half:, :], preferred_element_type=F32))
    act = (_silu(g) * u).astype(BF16)
    y_lo = jnp.dot(act, sd_s[:, :half], preferred_element_type=F32)
    y_hi = jnp.dot(act, sd_s[:, half:], preferred_element_type=F32)

    lax.fori_loop(0, tt, wait, 0)

    for k in range(TOP_K):
        e_lo, e_hi = _unpack_pair(buf[k])
        wk = wt_ref[:, k:k + 1]
        y_lo = y_lo + wk * e_lo
        y_hi = y_hi + wk * e_hi
    gate = gate_ref[...]
    out_ref[:, :half] = x_ref[:, :half] + gate[:, :half] * y_lo
    out_ref[:, half:] = x_ref[:, half:] + gate[:, half:] * y_hi


def combine_shared_residual(yp, dest, wt, hp, w_sg, w_su, w_sd, x, mod_rows, layer, which_gate):
    nb, S, D = x.shape
    T, half = hp.shape
    K = dest.shape[0]
    F = w_sg.shape[2]
    tt = min(128, S)
    nts = S // tt
    xrow = pl.BlockSpec((None, tt, D), lambda t: (t // nts, t % nts, 0))
    return pl.pallas_call(
        functools.partial(_combine_kernel, tt=tt),
        out_shape=jax.ShapeDtypeStruct((nb, S, D), F32),
        grid=(T // tt,),
        in_specs=[pl.BlockSpec((K, tt), lambda t: (0, t), memory_space=pltpu.SMEM),
                  pl.BlockSpec((tt, half), lambda t: (t, 0)),
                  pl.BlockSpec((tt, K), lambda t: (t, 0)),
                  pl.BlockSpec((None, D, F), lambda t: (layer, 0, 0)),
                  pl.BlockSpec((None, D, F), lambda t: (layer, 0, 0)),
                  pl.BlockSpec((None, F, D), lambda t: (layer, 0, 0)),
                  xrow,
                  pl.BlockSpec((None, 1, D), lambda t: ((layer * nb + t // nts) * N_MOD + which_gate, 0, 0)),
                  pl.BlockSpec(memory_space=pl.ANY)],
        out_specs=xrow,
        scratch_shapes=[pltpu.VMEM((K, tt, half), U32),
                        pltpu.VMEM((D, F), BF16), pltpu.VMEM((D, F), BF16), pltpu.VMEM((F, D), BF16),
                        pltpu.SemaphoreType.DMA(())],
        compiler_params=_params("arbitrary"),
        name="combine_shared_residual",
    )(dest, hp, wt, w_sg, w_su, w_sd, x, mod_rows, yp)


def _rmsnorm_kernel(x_ref, w_ref, out_ref):
    x = x_ref[...]
    out_ref[...] = x * lax.rsqrt(jnp.mean(x * x, axis=-1, keepdims=True) + EPS) * w_ref[...]


def rmsnorm(x, w):
    nb, S, D = x.shape
    ts = min(512, S)
    return pl.pallas_call(
        _rmsnorm_kernel,
        out_shape=jax.ShapeDtypeStruct((nb, S, D), F32),
        grid=(nb, S // ts),
        in_specs=[pl.BlockSpec((None, ts, D), lambda b, s: (b, s, 0)),
                  pl.BlockSpec((1, D), lambda b, s: (0, 0))],
        out_specs=pl.BlockSpec((None, ts, D), lambda b, s: (b, s, 0)),
        compiler_params=_params("parallel", "parallel"),
        name="final_rmsnorm",
    )(x, w.reshape(1, D))


def moe_block(x, norm_w, mod_rows, layer, w_router, router_bias, w_exp_gate, w_exp_up, w_exp_down,
              w_sh_gate, w_sh_up, w_sh_down):
    hp, logits = norm_router(x, norm_w, mod_rows, layer, w_router)
    eid, pos, wt, counts = route(logits, router_bias)
    counts = counts[:, 0]
    offsets = jnp.cumsum(counts) - counts
    dest = destination_rows(offsets, eid, pos)
    xp = dispatch_rows(hp, dest)
    segments = expert_segments(counts, xp.shape[0], min(EXPERT_TILE, xp.shape[0]))
    yp = routed_experts(xp, segments, w_exp_gate, w_exp_up, w_exp_down, layer)
    return combine_shared_residual(yp, dest, wt.T, hp, w_sh_gate, w_sh_up, w_sh_down, x, mod_rows, layer, 5)


def kernel(x, c, norm_mix, norm_ffn, w_ada, b_ada, w_in, hgrn_lower_bounds, hgrn_out_norm, w_proj_a, w_proj_b, w_out, w_router, router_bias, w_exp_gate, w_exp_up, w_exp_down, w_sh_gate, w_sh_up, w_sh_down, norm_final):
    nb, S, D = x.shape
    depth = w_in.shape[0]
    a_width = w_proj_a.shape[1]
    b_width = w_proj_b.shape[1]
    in_cols = w_in.shape[2]
    qkv_cols = 3 * a_width
    rest_cols = in_cols - qkv_cols
    col_qb = 0
    col_fb = col_qb + b_width
    col_ib = col_fb + b_width
    col_gb = col_ib + b_width
    col_gate_a = col_gb + b_width
    col_gate_b = col_gate_a + D
    dilations = tuple(d for _, d in DILATED_PATTERNS)
    assert all(w // d == SPAN for w, d in DILATED_PATTERNS)

    lb_sm = jax.nn.softmax(hgrn_lower_bounds.astype(F32), axis=0)
    lb_all = jnp.cumsum(lb_sm, axis=0) - lb_sm[0:1]
    cos, sin = rope_tables(S)

    mod = adaln_modulation(c, w_ada, b_ada)
    mod_rows = mod.reshape(depth * nb * N_MOD, 1, D)

    for l in range(depth):
        h = norm_modulate(x, norm_mix[:, None, :], mod_rows, l, 0, 1)
        qkvs = qkv_projection(h, w_in, l, qkv_cols, 2 * a_width, cos, sin, dilations)
        rest = matmul_cols(h.reshape(nb * S, D), w_in, l, qkv_cols, rest_cols, BF16)

        outs, lses = [], []
        for qkv in qkvs:
            o_g, lse_g = dilated_attention_branch(qkv, a_width)
            outs.append(o_g)
            lses.append(jnp.transpose(lse_g, (0, 2, 1, 3)).reshape(nb, S, -1))
        o_a = attention_merge(outs, lses, dilations)

        o_b = hgrn2(rest.reshape(nb, S, rest_cols), col_qb, col_fb, col_ib, col_gb, b_width,
                    lb_all[l][None, :], hgrn_out_norm[l][None, :])

        u = merge_projection(o_a.reshape(nb * S, a_width), o_b.reshape(nb * S, b_width),
                             w_proj_a, w_proj_b, l, rest, col_gate_a, col_gate_b)
        x = projection_residual(u.reshape(nb, S, D), w_out, x, mod_rows, l, 2)

        x = moe_block(x, norm_ffn[:, None, :], mod_rows, l, w_router[l], router_bias[l],
                      w_exp_gate, w_exp_up, w_exp_down, w_sh_gate, w_sh_up, w_sh_down)
    return rmsnorm(x, norm_final)
```

```python
import functools

import jax
import jax.numpy as jnp
import numpy as np
from jax import lax
from jax.experimental import pallas as pl
from jax.experimental.pallas import tpu as pltpu

HEAD_DIM = 128
DILATED_PATTERNS = ((128, 1), (512, 4), (2048, 16))
SPAN = 128
ROPE_THETA = 10000.0
N_EXPERTS = 64
TOP_K = 8
N_GROUPS = 8
TOPK_GROUPS = 4
GROUP_SIZE = N_EXPERTS // N_GROUPS
ROUTED_SCALE = 2.5
N_MOD = 6
EPS = 1e-6

LANES = 128
SUBLANES = 8
VMEM_LIMIT_BYTES = 56 * 1024 * 1024

HGRN_CHUNK = 128
HGRN_LEVELS = 7

EXPERT_TILE = 512

BF16 = jnp.bfloat16
F32 = jnp.float32
U32 = jnp.uint32
I32 = jnp.int32

_NT = (((1,), (1,)), ((), ()))
_TN = (((0,), (0,)), ((), ()))


def _params(*sem):
    return pltpu.CompilerParams(dimension_semantics=sem, vmem_limit_bytes=VMEM_LIMIT_BYTES)


def _sigmoid(x):
    return 1.0 / (1.0 + jnp.exp(-x))


def _silu(x):
    return x * _sigmoid(x)


def _split_bf16(x):
    hi = x.astype(BF16)
    lo = (x - hi.astype(F32)).astype(BF16)
    return hi, lo


def _bf16_bits(x):
    return (lax.bitcast_convert_type(x, U32) + jnp.uint32(0x8000)) & jnp.uint32(0xFFFF0000)


def _pack_pair(lo, hi):
    return ((lax.bitcast_convert_type(lo, U32) + jnp.uint32(0x8000)) >> 16) | _bf16_bits(hi)


def _unpack_pair(p):
    lo = lax.bitcast_convert_type(p << 16, F32)
    hi = lax.bitcast_convert_type(p & jnp.uint32(0xFFFF0000), F32)
    return lo, hi


def _adaln_kernel(cb_ref, w_ref, bias_ref, out_ref, cs_ref, *, kc):
    K, bn = w_ref.shape
    nb = cb_ref.shape[0]

    @pl.when((pl.program_id(0) == 0) & (pl.program_id(1) == 0))
    def _():
        cs_ref[...] = _silu(cb_ref[...])

    for j in range(bn // LANES):
        cols = slice(j * LANES, (j + 1) * LANES)

        def body(i, accs):
            k0 = pl.multiple_of(i * kc, kc)
            w = w_ref[pl.ds(k0, kc), cols]
            out = []
            for b in range(nb):
                p = (w * cs_ref[b, pl.ds(k0, kc), :]).reshape(kc // SUBLANES, SUBLANES, LANES)
                out.append(accs[b] + jnp.sum(p, axis=0))
            return tuple(out)

        accs = lax.fori_loop(0, K // kc, body,
                             tuple(jnp.zeros((SUBLANES, LANES), F32) for _ in range(nb)))
        for b in range(nb):
            out_ref[b:b + 1, cols] = jnp.sum(accs[b], axis=0, keepdims=True) + bias_ref[:, cols]


def adaln_modulation(c, w_ada, b_ada):
    nl, K, N = w_ada.shape
    nb = c.shape[0]
    bn = min(512, N)
    kc = min(64, K)
    cb = jnp.broadcast_to(c[:, :, None], (nb, K, LANES))
    return pl.pallas_call(
        functools.partial(_adaln_kernel, kc=kc),
        out_shape=jax.ShapeDtypeStruct((nl, nb, N), F32),
        grid=(nl, N // bn),
        in_specs=[
            pl.BlockSpec((nb, K, LANES), lambda l, n: (0, 0, 0)),
            pl.BlockSpec((None, K, bn), lambda l, n: (l, 0, n)),
            pl.BlockSpec((None, 1, bn), lambda l, n: (l, 0, n)),
        ],
        out_specs=pl.BlockSpec((None, nb, bn), lambda l, n: (l, 0, n)),
        scratch_shapes=[pltpu.VMEM((nb, K, LANES), F32)],
        compiler_params=_params("arbitrary", "arbitrary"),
        name="adaln_modulation",
    )(cb, w_ada, b_ada.reshape(nl, 1, N))


def _norm_mod(x, w, shift, scale):
    y = x * lax.rsqrt(jnp.mean(x * x, axis=-1, keepdims=True) + EPS) * w
    return y * (1.0 + scale) + shift


def _norm_mod_kernel(x_ref, w_ref, shift_ref, scale_ref, out_ref):
    out_ref[...] = _norm_mod(x_ref[...], w_ref[...], shift_ref[...], scale_ref[...]).astype(out_ref.dtype)


def _mod_spec(D, which, layer, nb):
    return pl.BlockSpec((None, 1, D), lambda b, *_: ((layer * nb + b) * N_MOD + which, 0, 0))


def norm_modulate(x, norm_w, mod_rows, layer, which_shift, which_scale):
    nb, S, D = x.shape
    ts = min(512, S)
    return pl.pallas_call(
        _norm_mod_kernel,
        out_shape=jax.ShapeDtypeStruct((nb, S, D), BF16),
        grid=(nb, S // ts),
        in_specs=[
            pl.BlockSpec((None, ts, D), lambda b, s: (b, s, 0)),
            pl.BlockSpec((None, 1, D), lambda b, s: (layer, 0, 0)),
            _mod_spec(D, which_shift, layer, nb),
            _mod_spec(D, which_scale, layer, nb),
        ],
        out_specs=pl.BlockSpec((None, ts, D), lambda b, s: (b, s, 0)),
        compiler_params=_params("parallel", "parallel"),
        name="norm_modulate",
    )(x, norm_w, mod_rows, mod_rows)


def _qkv_proj_kernel(x_ref, w_ref, cos_ref, sin_ref, *refs, n_rope_tiles, dilations):
    out_refs, scr = refs[:-1], refs[-1]
    n_heads, bm, _ = scr.shape
    y = jnp.dot(x_ref[...], w_ref[...].astype(BF16), preferred_element_type=F32)
    n = pl.program_id(2)

    @pl.when(n < n_rope_tiles)
    def _():
        cos = cos_ref[...]
        sin = sin_ref[...]
        for h in range(n_heads):
            t = y[:, h * HEAD_DIM:(h + 1) * HEAD_DIM]
            scr[h] = t * cos + pltpu.roll(t, HEAD_DIM // 2, 1) * sin

    @pl.when(n >= n_rope_tiles)
    def _():
        for h in range(n_heads):
            scr[h] = y[:, h * HEAD_DIM:(h + 1) * HEAD_DIM]

    for d, o_ref in zip(dilations, out_refs):
        for r in range(d):
            for h in range(n_heads):
                o_ref[r, :, h * HEAD_DIM:(h + 1) * HEAD_DIM] = (
                    scr[h, pl.ds(r, bm // d, stride=d), :].astype(o_ref.dtype))


def rope_tables(S):
    half = HEAD_DIM // 2
    inv = ROPE_THETA ** (-jnp.arange(half, dtype=F32) / half)
    ang = jnp.arange(S, dtype=F32)[:, None] * inv[None, :]
    cos, sin = jnp.cos(ang), jnp.sin(ang)
    return jnp.concatenate([cos, cos], axis=-1), jnp.concatenate([-sin, sin], axis=-1)


def qkv_projection(h, w_in, layer, qkv_cols, rope_cols, cos, sin, dilations):
    nb, S, D = h.shape
    bm, bn = min(1024, S), min(512, qkv_cols)
    outs = tuple(jax.ShapeDtypeStruct((nb, d, S // d, qkv_cols), BF16) for d in dilations)
    out_specs = tuple(pl.BlockSpec((None, d, bm // d, bn), lambda b, m, n: (b, 0, m, n)) for d in dilations)
    return pl.pallas_call(
        functools.partial(_qkv_proj_kernel, n_rope_tiles=rope_cols // bn, dilations=dilations),
        out_shape=outs,
        grid=(nb, S // bm, qkv_cols // bn),
        in_specs=[pl.BlockSpec((None, bm, D), lambda b, m, n: (b, m, 0)),
                  pl.BlockSpec((None, D, bn), lambda b, m, n: (layer, 0, n)),
                  pl.BlockSpec((bm, HEAD_DIM), lambda b, m, n: (m, 0)),
                  pl.BlockSpec((bm, HEAD_DIM), lambda b, m, n: (m, 0))],
        out_specs=out_specs,
        scratch_shapes=[pltpu.VMEM((bn // HEAD_DIM, bm, HEAD_DIM), F32)],
        compiler_params=_params("parallel", "parallel", "arbitrary"),
        name="qkv_projection",
    )(h, w_in, cos, sin)


def _matmul_kernel(x_ref, w_ref, out_ref):
    out_ref[...] = jnp.dot(x_ref[...], w_ref[...].astype(BF16),
                           preferred_element_type=F32).astype(out_ref.dtype)


def matmul_cols(x, w, layer, col0, ncols, out_dtype):
    M, K = x.shape
    bm, bn = min(1024, M), min(512, ncols)
    return pl.pallas_call(
        _matmul_kernel,
        out_shape=jax.ShapeDtypeStruct((M, ncols), out_dtype),
        grid=(M // bm, ncols // bn),
        in_specs=[pl.BlockSpec((bm, K), lambda m, n: (m, 0)),
                  pl.BlockSpec((None, K, bn), lambda m, n: (layer, 0, col0 // bn + n))],
        out_specs=pl.BlockSpec((bm, bn), lambda m, n: (m, n)),
        compiler_params=_params("parallel", "arbitrary"),
        name="matmul_cols",
    )(x, w)


def _attn_kernel(q_ref, kp_ref, kc_ref, vp_ref, vc_ref, o_ref, lse_ref):
    i = pl.program_id(2)
    n_heads = q_ref.shape[-1] // HEAD_DIM
    qi = lax.broadcasted_iota(I32, (SPAN, 2 * SPAN), 0)
    kj = lax.broadcasted_iota(I32, (SPAN, 2 * SPAN), 1)
    mask = ((kj < SPAN) & (kj >= qi) & (i > 0)) | ((kj >= SPAN) & ((kj - SPAN) <= qi))
    scale = HEAD_DIM ** -0.5
    s = []
    for h in range(n_heads):
        cols = slice(h * HEAD_DIM, (h + 1) * HEAD_DIM)
        k_h = jnp.concatenate([kp_ref[:, cols], kc_ref[:, cols]], axis=0)
        s.append(lax.dot_general(q_ref[:, cols], k_h, _NT, preferred_element_type=F32))
    s = jnp.stack(s, axis=0)
    s = jnp.where(mask[None], s * scale, -jnp.inf)
    m = jnp.max(s, axis=-1, keepdims=True)
    p = jnp.exp(s - m).astype(BF16)
    ones = jnp.ones((2 * SPAN, HEAD_DIM), BF16)
    lses = []
    for h in range(n_heads):
        cols = slice(h * HEAD_DIM, (h + 1) * HEAD_DIM)
        v_h = jnp.concatenate([vp_ref[:, cols], vc_ref[:, cols]], axis=0)
        oe = jnp.dot(p[h], jnp.concatenate([v_h, ones], axis=1), preferred_element_type=F32)
        den = oe[:, HEAD_DIM:]
        o_ref[:, cols] = (oe[:, :HEAD_DIM] / den).astype(o_ref.dtype)
        lses.append(m[h] + jnp.log(den[:, :1]))
    lse_ref[...] = jnp.concatenate(lses, axis=-1)


def dilated_attention_branch(qkv, width):
    nb, d, L, _ = qkv.shape
    nblk = L // SPAN
    n_heads = width // HEAD_DIM

    def spec(col, prev):
        if prev:
            return pl.BlockSpec((None, None, SPAN, width), lambda b, r, i: (b, r, jnp.maximum(i - 1, 0), col))
        return pl.BlockSpec((None, None, SPAN, width), lambda b, r, i: (b, r, i, col))

    return pl.pallas_call(
        _attn_kernel,
        out_shape=(jax.ShapeDtypeStruct((nb, d, L, width), BF16),
                   jax.ShapeDtypeStruct((nb, d, L, n_heads), F32)),
        grid=(nb, d, nblk),
        in_specs=[spec(0, False), spec(1, True), spec(1, False), spec(2, True), spec(2, False)],
        out_specs=(pl.BlockSpec((None, None, SPAN, width), lambda b, r, i: (b, r, i, 0)),
                   pl.BlockSpec((None, None, SPAN, n_heads), lambda b, r, i: (b, r, i, 0))),
        compiler_params=_params("parallel", "parallel", "arbitrary"),
        name=f"dilated_attention_d{d}",
    )(qkv, qkv, qkv, qkv, qkv)


def _attn_merge_kernel(*refs, dilations):
    n = len(dilations)
    o_refs, lse_refs, out_ref, scr_refs = refs[:n], refs[n:2 * n], refs[2 * n], refs[2 * n + 1:]
    ts = out_ref.shape[0]
    nat = []
    for g, d in enumerate(dilations):
        if d == 1:
            nat.append(None)
            continue
        scr = scr_refs[len([x for x in nat if x is not None])]
        for r in range(d):
            for h in range(scr.shape[0]):
                scr[h, pl.ds(r, ts // d, stride=d), :] = (
                    o_refs[g][r, :, h * HEAD_DIM:(h + 1) * HEAD_DIM].astype(F32))
        nat.append(scr)
    lses = [r[...] for r in lse_refs]
    m = functools.reduce(jnp.maximum, lses)
    es = [jnp.exp(l - m) for l in lses]
    tot = functools.reduce(jnp.add, es)
    ws = [e / tot for e in es]
    for h in range(out_ref.shape[-1] // HEAD_DIM):
        cols = slice(h * HEAD_DIM, (h + 1) * HEAD_DIM)
        acc = jnp.zeros((ts, HEAD_DIM), F32)
        for g in range(n):
            o = o_refs[g][0, :, cols].astype(F32) if nat[g] is None else nat[g][h]
            acc = acc + ws[g][:, h:h + 1] * o
        out_ref[:, cols] = acc.astype(out_ref.dtype)


def attention_merge(outs, lses, dilations):
    nb, _, S, width = outs[dilations.index(1)].shape
    n_heads = lses[0].shape[-1]
    ts = min(512, S)
    o_specs = [pl.BlockSpec((None, d, ts // d, width), lambda b, s: (b, 0, s, 0)) for d in dilations]
    l_spec = pl.BlockSpec((None, ts, n_heads), lambda b, s: (b, s, 0))
    return pl.pallas_call(
        functools.partial(_attn_merge_kernel, dilations=dilations),
        out_shape=jax.ShapeDtypeStruct((nb, S, width), BF16),
        grid=(nb, S // ts),
        in_specs=o_specs + [l_spec] * len(lses),
        out_specs=pl.BlockSpec((None, ts, width), lambda b, s: (b, s, 0)),
        scratch_shapes=[pltpu.VMEM((n_heads, ts, HEAD_DIM), F32) for d in dilations if d > 1],
        compiler_params=_params("parallel", "parallel"),
        name="attention_merge",
    )(*outs, *lses)


def _hgrn_sum_matrix(C, levels):
    t = np.arange(C)[:, None]
    u = np.arange(C)[None, :]
    blocks = [(u <= t), (u > t)]
    for j in range(levels):
        half = C >> (j + 1)
        mid = (t // (2 * half)) * (2 * half) + half - 1
        upper = (t // half) % 2 == 1
        blocks.append(np.where(upper, (u > mid) & (u <= t), (u > t) & (u <= mid)))
    return np.concatenate(blocks, axis=0).astype(np.float32)


def _hgrn_kernel(q_ref, f_ref, i_ref, g_ref, lb_ref, nw_ref, sm_ref, out_ref, state_ref, *, n_chunks):
    C = HGRN_CHUNK

    @pl.when(pl.program_id(2) == 0)
    def _():
        state_ref[...] = jnp.zeros_like(state_ref)

    ti = lax.broadcasted_iota(I32, (C, C), 0)
    si = lax.broadcasted_iota(I32, (C, C), 1)
    xor = ti ^ si
    lower = si < ti
    lb = lb_ref[...]
    nw = nw_ref[...]
    ones = jnp.ones((C, LANES), BF16)

    for c in range(n_chunks):
        rows = slice(c * C, (c + 1) * C)
        q = _silu(q_ref[rows, :].astype(F32))
        f = lb + (1.0 - lb) * _sigmoid(f_ref[rows, :].astype(F32))
        k = 1.0 - f
        g = jnp.log(f)
        v = i_ref[rows, :]
        g_hi, g_lo = _split_bf16(g)
        ghl = jnp.concatenate([g_hi, g_lo], axis=-1)
        e2 = jnp.dot(sm_ref[...], ghl, preferred_element_type=F32)
        e = e2[:, :HEAD_DIM] + e2[:, HEAD_DIM:]
        b_end = (lax.dot_general(g_hi, ones, _TN, preferred_element_type=F32)
                 + lax.dot_general(g_lo, ones, _TN, preferred_element_type=F32))

        scores = jnp.where(ti == si, lax.dot_general(q.astype(BF16), k.astype(BF16), _NT,
                                                     preferred_element_type=F32), 0.0)
        for j in range(HGRN_LEVELS):
            a = jnp.exp(e[(2 + j) * C:(3 + j) * C, :])
            s_j = lax.dot_general((q * a).astype(BF16), (k * a).astype(BF16), _NT,
                                  preferred_element_type=F32)
            scores = scores + jnp.where(lower & ((xor >> (HGRN_LEVELS - 1 - j)) == 1), s_j, 0.0)

        state = state_ref[...]
        o = jnp.dot(scores.astype(BF16), v, preferred_element_type=F32)
        o = o + jnp.dot((q * jnp.exp(e[0:C, :])).astype(BF16), state.astype(BF16),
                        preferred_element_type=F32)
        k_end = (k * jnp.exp(e[C:2 * C, :])).astype(BF16)
        state_ref[...] = jnp.exp(b_end) * state + lax.dot_general(k_end, v, _TN, preferred_element_type=F32)

        y = o * lax.rsqrt(jnp.mean(o * o, axis=-1, keepdims=True) + EPS) * nw
        out_ref[rows, :] = (y * _silu(g_ref[rows, :].astype(F32))).astype(out_ref.dtype)


def hgrn2(proj, col_q, col_f, col_i, col_g, width, lb, out_norm):
    nb, S, _ = proj.shape
    n_heads = width // HEAD_DIM
    C = HGRN_CHUNK
    ts = min(512, S)
    sm = jnp.asarray(_hgrn_sum_matrix(C, HGRN_LEVELS), dtype=BF16)

    def col_spec(col0):
        blk = col0 // HEAD_DIM
        return pl.BlockSpec((None, ts, HEAD_DIM), lambda b, h, s: (b, s, blk + h))

    return pl.pallas_call(
        functools.partial(_hgrn_kernel, n_chunks=ts // C),
        out_shape=jax.ShapeDtypeStruct((nb, S, width), BF16),
        grid=(nb, n_heads, S // ts),
        in_specs=[col_spec(col_q), col_spec(col_f), col_spec(col_i), col_spec(col_g),
                  pl.BlockSpec((1, HEAD_DIM), lambda b, h, s: (0, h)),
                  pl.BlockSpec((1, HEAD_DIM), lambda b, h, s: (0, 0)),
                  pl.BlockSpec(sm.shape, lambda b, h, s: (0, 0))],
        out_specs=pl.BlockSpec((None, ts, HEAD_DIM), lambda b, h, s: (b, s, h)),
        scratch_shapes=[pltpu.VMEM((HEAD_DIM, HEAD_DIM), F32)],
        compiler_params=_params("parallel", "parallel", "arbitrary"),
        name="hgrn2",
    )(proj, proj, proj, proj, lb, out_norm, sm)


def _merge_proj_kernel(oa_ref, ob_ref, wa_ref, wb_ref, ga_ref, gb_ref, out_ref):
    ya = jnp.dot(oa_ref[...], wa_ref[...].astype(BF16), preferred_element_type=F32)
    yb = jnp.dot(ob_ref[...], wb_ref[...].astype(BF16), preferred_element_type=F32)
    u = _sigmoid(ga_ref[...].astype(F32)) * ya + _sigmoid(gb_ref[...].astype(F32)) * yb
    out_ref[...] = u.astype(out_ref.dtype)


def merge_projection(o_a, o_b, w_a, w_b, layer, proj, col_ga, col_gb):
    M, K = o_a.shape
    N = w_a.shape[2]
    bm, bn = min(1024, M), min(512, N)
    return pl.pallas_call(
        _merge_proj_kernel,
        out_shape=jax.ShapeDtypeStruct((M, N), BF16),
        grid=(M // bm, N // bn),
        in_specs=[pl.BlockSpec((bm, K), lambda m, n: (m, 0)),
                  pl.BlockSpec((bm, K), lambda m, n: (m, 0)),
                  pl.BlockSpec((None, K, bn), lambda m, n: (layer, 0, n)),
                  pl.BlockSpec((None, K, bn), lambda m, n: (layer, 0, n)),
                  pl.BlockSpec((bm, bn), lambda m, n: (m, col_ga // bn + n)),
                  pl.BlockSpec((bm, bn), lambda m, n: (m, col_gb // bn + n))],
        out_specs=pl.BlockSpec((bm, bn), lambda m, n: (m, n)),
        compiler_params=_params("parallel", "arbitrary"),
        name="merge_projection",
    )(o_a, o_b, w_a, w_b, proj, proj)


def _proj_residual_kernel(u_ref, w_ref, x_ref, gate_ref, out_ref):
    y = jnp.dot(u_ref[...], w_ref[...].astype(BF16), preferred_element_type=F32)
    out_ref[...] = x_ref[...] + gate_ref[...] * y


def projection_residual(u, w, x, mod_rows, layer, which_gate):
    nb, S, K = u.shape
    D = w.shape[2]
    bm, bn = min(1024, S), min(512, D)
    return pl.pallas_call(
        _proj_residual_kernel,
        out_shape=jax.ShapeDtypeStruct((nb, S, D), F32),
        grid=(nb, S // bm, D // bn),
        in_specs=[pl.BlockSpec((None, bm, K), lambda b, m, n: (b, m, 0)),
                  pl.BlockSpec((None, K, bn), lambda b, m, n: (layer, 0, n)),
                  pl.BlockSpec((None, bm, bn), lambda b, m, n: (b, m, n)),
                  pl.BlockSpec((None, 1, bn), lambda b, m, n: ((layer * nb + b) * N_MOD + which_gate, 0, n))],
        out_specs=pl.BlockSpec((None, bm, bn), lambda b, m, n: (b, m, n)),
        compiler_params=_params("parallel", "parallel", "arbitrary"),
        name="projection_residual",
    )(u, w, x, mod_rows)


def _norm_router_kernel(x_ref, w_ref, shift_ref, scale_ref, wr_hi_ref, wr_lo_ref, hp_ref, logit_ref):
    h = _norm_mod(x_ref[...], w_ref[...], shift_ref[...], scale_ref[...])
    half = h.shape[1] // 2
    bits = _bf16_bits(h)
    hp_ref[...] = (bits[:, :half] >> 16) | bits[:, half:]
    h_r = lax.bitcast_convert_type(bits, F32)
    h_hi = h_r.astype(BF16)
    h_lo = (h - h_r).astype(BF16)
    logit_ref[...] = (lax.dot_general(wr_hi_ref[...], h_hi, _NT, preferred_element_type=F32)
                      + lax.dot_general(wr_hi_ref[...], h_lo, _NT, preferred_element_type=F32)
                      + lax.dot_general(wr_lo_ref[...], h_hi, _NT, preferred_element_type=F32))


def norm_router(x, norm_w, mod_rows, layer, w_router):
    nb, S, D = x.shape
    E = w_router.shape[1]
    ts = min(512, S)
    nts = S // ts
    wr_hi, wr_lo = _split_bf16(w_router.T)
    return pl.pallas_call(
        _norm_router_kernel,
        out_shape=(jax.ShapeDtypeStruct((nb * S, D // 2), U32), jax.ShapeDtypeStruct((E, nb * S), F32)),
        grid=(nb, nts),
        in_specs=[pl.BlockSpec((None, ts, D), lambda b, s: (b, s, 0)),
                  pl.BlockSpec((None, 1, D), lambda b, s: (layer, 0, 0)),
                  _mod_spec(D, 3, layer, nb),
                  _mod_spec(D, 4, layer, nb),
                  pl.BlockSpec((E, D), lambda b, s: (0, 0)),
                  pl.BlockSpec((E, D), lambda b, s: (0, 0))],
        out_specs=(pl.BlockSpec((ts, D // 2), lambda b, s: (b * nts + s, 0)),
                   pl.BlockSpec((E, ts), lambda b, s: (0, b * nts + s))),
        compiler_params=_params("parallel", "parallel"),
        name="norm_router",
    )(x, norm_w, mod_rows, mod_rows, wr_hi, wr_lo)


def _rank_lt(vals, n_rows, limit):
    ridx = lax.broadcasted_iota(I32, vals.shape, 0)
    cnt = jnp.zeros(vals.shape, I32)
    for r in range(n_rows):
        other = vals[r:r + 1, :]
        beats = (other > vals) | ((other == vals) & (r < ridx))
        cnt = cnt + beats.astype(I32)
    return cnt < limit


def _route_kernel(logit_ref, bias_ref, trie_ref, trit_ref, eid_ref, pos_ref, wt_ref, cnt_ref, carry_ref):
    @pl.when(pl.program_id(0) == 0)
    def _():
        carry_ref[...] = jnp.zeros_like(carry_ref)

    scores = _sigmoid(logit_ref[...])
    sel = scores + bias_ref[...]
    E, T = sel.shape
    sub = lax.broadcasted_iota(I32, (GROUP_SIZE, T), 0)
    gscores = []
    for g in range(N_GROUPS):
        v = sel[g * GROUP_SIZE:(g + 1) * GROUP_SIZE, :]
        m1 = jnp.max(v, axis=0, keepdims=True)
        first = jnp.min(jnp.where(v == m1, sub, GROUP_SIZE), axis=0, keepdims=True)
        m2 = jnp.max(jnp.where(sub == first, -jnp.inf, v), axis=0, keepdims=True)
        gscores.append(m1 + m2)
    gsc = jnp.concatenate(gscores, axis=0)
    gkeep = _rank_lt(gsc, N_GROUPS, TOPK_GROUPS)
    ekeep = jnp.concatenate(
        [jnp.broadcast_to(gkeep[g:g + 1, :], (GROUP_SIZE, T)) for g in range(N_GROUPS)], axis=0)
    masked = jnp.where(ekeep, sel, -jnp.inf)
    chosen = _rank_lt(masked, N_EXPERTS, TOP_K)
    w = jnp.where(chosen, scores, 0.0)
    combine = w / jnp.sum(w, axis=0, keepdims=True) * ROUTED_SCALE

    cf = jnp.where(chosen, 1.0, 0.0)
    cb = cf.astype(BF16)
    rank = jnp.dot(trie_ref[...], cb, preferred_element_type=F32)
    local = jnp.dot(cb, trit_ref[...], preferred_element_type=F32)
    carry = carry_ref[:, :1]
    pos = carry + local
    eidx = lax.broadcasted_iota(I32, (E, T), 0).astype(F32)
    eids, poss, wts = [], [], []
    for k in range(TOP_K):
        sel_k = chosen & (rank == k)
        eids.append(jnp.sum(jnp.where(sel_k, eidx, 0.0), axis=0, keepdims=True))
        poss.append(jnp.sum(jnp.where(sel_k, pos, 0.0), axis=0, keepdims=True))
        wts.append(jnp.sum(jnp.where(sel_k, combine, 0.0), axis=0, keepdims=True))
    eid_ref[...] = jnp.concatenate(eids, axis=0).astype(I32)
    pos_ref[...] = jnp.concatenate(poss, axis=0).astype(I32)
    wt_ref[...] = jnp.concatenate(wts, axis=0)
    total = carry + jnp.sum(cf, axis=1, keepdims=True)
    carry_ref[...] = jnp.broadcast_to(total, carry_ref.shape)
    cnt_ref[...] = jnp.broadcast_to(total, cnt_ref.shape).astype(I32)


def route(logits, router_bias):
    E, T = logits.shape
    tt = min(512, T)
    trie = jnp.asarray(np.tril(np.ones((E, E), np.float32), -1), BF16)
    trit = jnp.asarray(np.triu(np.ones((tt, tt), np.float32), 1), BF16)
    kt = pl.BlockSpec((TOP_K, tt), lambda t: (0, t))
    return pl.pallas_call(
        _route_kernel,
        out_shape=(jax.ShapeDtypeStruct((TOP_K, T), I32), jax.ShapeDtypeStruct((TOP_K, T), I32),
                   jax.ShapeDtypeStruct((TOP_K, T), F32), jax.ShapeDtypeStruct((E, LANES), I32)),
        grid=(T // tt,),
        in_specs=[pl.BlockSpec((E, tt), lambda t: (0, t)),
                  pl.BlockSpec((E, 1), lambda t: (0, 0)),
                  pl.BlockSpec((E, E), lambda t: (0, 0)),
                  pl.BlockSpec((tt, tt), lambda t: (0, 0))],
        out_specs=(kt, kt, kt, pl.BlockSpec((E, LANES), lambda t: (0, 0))),
        scratch_shapes=[pltpu.VMEM((E, LANES), F32)],
        compiler_params=_params("arbitrary"),
        name="route",
    )(logits, router_bias.reshape(E, 1), trie, trit)


def _dest_kernel(off_ref, eid_ref, pos_ref, dest_ref):
    eid = eid_ref[...]
    acc = pos_ref[...]
    for e in range(N_EXPERTS):
        acc = acc + jnp.where(eid == e, off_ref[e], 0)
    dest_ref[...] = acc


def destination_rows(offsets, eid, pos):
    K, T = eid.shape
    tt = min(2048, T)
    kt = pl.BlockSpec((K, tt), lambda t, off: (0, t))
    return pl.pallas_call(
        _dest_kernel,
        out_shape=jax.ShapeDtypeStruct((K, T), I32),
        grid_spec=pltpu.PrefetchScalarGridSpec(num_scalar_prefetch=1, grid=(T // tt,),
                                               in_specs=[kt, kt], out_specs=kt),
        compiler_params=_params("parallel"),
        name="destination_rows",
    )(offsets, eid, pos)


def _dispatch_kernel(tok_ref, h_hbm, xp_ref, sem):
    tm = xp_ref.shape[0]

    def row_copy(i):
        return pltpu.make_async_copy(h_hbm.at[pl.ds(tok_ref[0, i], 1)], xp_ref.at[pl.ds(i, 1)], sem)

    def start(i, carry):
        row_copy(i).start()
        return carry

    def wait(i, carry):
        row_copy(i).wait()
        return carry

    lax.fori_loop(0, tm, start, 0, unroll=8)
    lax.fori_loop(0, tm, wait, 0, unroll=8)


def dispatch_rows(hp, token_of_row):
    T, W = hp.shape
    R = token_of_row.shape[1]
    tm = min(512, R)
    return pl.pallas_call(
        _dispatch_kernel,
        out_shape=jax.ShapeDtypeStruct((R, W), U32),
        grid=(R // tm,),
        in_specs=[pl.BlockSpec((1, tm), lambda j: (0, j), memory_space=pltpu.SMEM),
                  pl.BlockSpec(memory_space=pl.ANY)],
        out_specs=pl.BlockSpec((tm, W), lambda j: (j, 0)),
        scratch_shapes=[pltpu.SemaphoreType.DMA(())],
        compiler_params=_params("arbitrary"),
        name="dispatch_rows",
    )(token_of_row, hp)


def _new_expert(eid_ref):
    v = pl.program_id(0)
    return (v == 0) | (eid_ref[v] != eid_ref[jnp.maximum(v - 1, 0)])


def _write_rows(out_ref, vals, tile_ref, lo_ref, hi_ref, first_ref):
    v = pl.program_id(0)
    tm = out_ref.shape[0]
    row = tile_ref[v] * tm + lax.broadcasted_iota(I32, (tm, 1), 0)
    mine = (row >= lo_ref[v]) & (row < hi_ref[v])

    @pl.when(first_ref[v] == 1)
    def _():
        out_ref[...] = jnp.where(mine, vals, jnp.zeros_like(vals))

    @pl.when(first_ref[v] == 0)
    def _():
        out_ref[...] = jnp.where(mine, vals, out_ref[...])


def _experts_up_kernel(tile_ref, eid_ref, lo_ref, hi_ref, first_ref, x_ref, wg_ref, wu_ref, act_ref,
                       wg_s, wu_s):
    @pl.when(_new_expert(eid_ref))
    def _():
        wg_s[...] = wg_ref[...].astype(BF16)
        wu_s[...] = wu_ref[...].astype(BF16)

    v = pl.program_id(0)

    @pl.when(hi_ref[v] > lo_ref[v])
    def _():
        half = x_ref.shape[1]
        x_lo, x_hi = _unpack_pair(x_ref[...])
        x_lo, x_hi = x_lo.astype(BF16), x_hi.astype(BF16)
        gate = (jnp.dot(x_lo, wg_s[:half, :], preferred_element_type=F32)
                + jnp.dot(x_hi, wg_s[half:, :], preferred_element_type=F32))
        up = (jnp.dot(x_lo, wu_s[:half, :], preferred_element_type=F32)
              + jnp.dot(x_hi, wu_s[half:, :], preferred_element_type=F32))
        _write_rows(act_ref, _silu(gate) * up, tile_ref, lo_ref, hi_ref, first_ref)


def _experts_down_kernel(tile_ref, eid_ref, lo_ref, hi_ref, first_ref, act_ref, wd_ref, y_ref, wd_s):
    @pl.when(_new_expert(eid_ref))
    def _():
        wd_s[...] = wd_ref[...].astype(BF16)

    v = pl.program_id(0)

    @pl.when(hi_ref[v] > lo_ref[v])
    def _():
        half = y_ref.shape[1]
        act = act_ref[...].astype(BF16)
        packed = _pack_pair(jnp.dot(act, wd_s[:, :half], preferred_element_type=F32),
                            jnp.dot(act, wd_s[:, half:], preferred_element_type=F32))
        _write_rows(y_ref, packed, tile_ref, lo_ref, hi_ref, first_ref)


def expert_segments(counts, n_rows, tm):
    E = counts.shape[0]
    n_tiles = n_rows // tm
    ends = jnp.cumsum(counts)
    starts = ends - counts
    cuts = jnp.sort(jnp.concatenate([jnp.arange(n_tiles, dtype=I32) * tm, starts.astype(I32)]))
    lo = cuts
    hi = jnp.concatenate([cuts[1:], jnp.array([n_rows], I32)])
    tile = jnp.minimum(lo // tm, n_tiles - 1)
    eid = jnp.minimum(jnp.sum(ends[None, :] <= lo[:, None], axis=1), E - 1).astype(I32)
    first = ((lo % tm == 0) & (hi > lo)).astype(I32)
    return tile.astype(I32), eid, lo.astype(I32), hi.astype(I32), first


def routed_experts(xp, segments, w_gate, w_up, w_down, layer):
    R, half = xp.shape
    _, E, D, F = w_gate.shape
    tm = min(EXPERT_TILE, R)
    n_visits = segments[0].shape[0]

    def tile_spec(width):
        return pl.BlockSpec((tm, width), lambda v, tile, eid, lo, hi, first: (tile[v], 0))

    def weight_spec(rows, cols):
        return pl.BlockSpec((None, None, rows, cols), lambda v, tile, eid, lo, hi, first: (layer, eid[v], 0, 0))

    act = pl.pallas_call(
        _experts_up_kernel,
        out_shape=jax.ShapeDtypeStruct((R, F), F32),
        grid_spec=pltpu.PrefetchScalarGridSpec(
            num_scalar_prefetch=5, grid=(n_visits,),
            in_specs=[tile_spec(half), weight_spec(D, F), weight_spec(D, F)],
            out_specs=tile_spec(F),
            scratch_shapes=[pltpu.VMEM((D, F), BF16), pltpu.VMEM((D, F), BF16)]),
        compiler_params=_params("arbitrary"),
        name="routed_experts_up",
    )(*segments, xp, w_gate, w_up)
    return pl.pallas_call(
        _experts_down_kernel,
        out_shape=jax.ShapeDtypeStruct((R, half), U32),
        grid_spec=pltpu.PrefetchScalarGridSpec(
            num_scalar_prefetch=5, grid=(n_visits,),
            in_specs=[tile_spec(F), weight_spec(F, D)],
            out_specs=tile_spec(half),
            scratch_shapes=[pltpu.VMEM((F, D), BF16)]),
        compiler_params=_params("arbitrary"),
        name="routed_experts_down",
    )(*segments, act, w_down)


def _combine_kernel(dest_ref, hp_ref, wt_ref, sg_ref, su_ref, sd_ref, x_ref, gate_ref, yp_hbm, out_ref,
                    buf, sg_s, su_s, sd_s, sem, *, tt):
    @pl.when(pl.program_id(0) == 0)
    def _():
        sg_s[...] = sg_ref[...].astype(BF16)
        su_s[...] = su_ref[...].astype(BF16)
        sd_s[...] = sd_ref[...].astype(BF16)

    def row_copy(i, k):
        return pltpu.make_async_copy(yp_hbm.at[pl.ds(dest_ref[k, i], 1)], buf.at[k, pl.ds(i, 1)], sem)

    def start(i, carry):
        for k in range(TOP_K):
            row_copy(i, k).start()
        return carry

    def wait(i, carry):
        for k in range(TOP_K):
            row_copy(i, k).wait()
        return carry

    lax.fori_loop(0, tt, start, 0)

    half = hp_ref.shape[1]
    h_lo, h_hi = _unpack_pair(hp_ref[...])
    h_lo, h_hi = h_lo.astype(BF16), h_hi.astype(BF16)
    g = (jnp.dot(h_lo, sg_s[:half, :], preferred_element_type=F32)
         + jnp.dot(h_hi, sg_s[half:, :], preferred_element_type=F32))
    u = (jnp.dot(h_lo, su_s[:half, :], preferred_element_type=F32)
         + jnp.dot(h_hi, su_s[half:, :], preferred_element_type=F32))
    act = (_silu(g) * u).astype(BF16)
    y_lo = jnp.dot(act, sd_s[:, :half], preferred_element_type=F32)
    y_hi = jnp.dot(act, sd_s[:, half:], preferred_element_type=F32)

    lax.fori_loop(0, tt, wait, 0)

    for k in range(TOP_K):
        e_lo, e_hi = _unpack_pair(buf[k])
        wk = wt_ref[:, k:k + 1]
        y_lo = y_lo + wk * e_lo
        y_hi = y_hi + wk * e_hi
    gate = gate_ref[...]
    out_ref[:, :half] = x_ref[:, :half] + gate[:, :half] * y_lo
    out_ref[:, half:] = x_ref[:, half:] + gate[:, half:] * y_hi


def combine_shared_residual(yp, dest, wt, hp, w_sg, w_su, w_sd, x, mod_rows, layer, which_gate):
    nb, S, D = x.shape
    T, half = hp.shape
    K = dest.shape[0]
    F = w_sg.shape[2]
    tt = min(128, S)
    nts = S // tt
    xrow = pl.BlockSpec((None, tt, D), lambda t: (t // nts, t % nts, 0))
    return pl.pallas_call(
        functools.partial(_combine_kernel, tt=tt),
        out_shape=jax.ShapeDtypeStruct((nb, S, D), F32),
        grid=(T // tt,),
        in_specs=[pl.BlockSpec((K, tt), lambda t: (0, t), memory_space=pltpu.SMEM),
                  pl.BlockSpec((tt, half), lambda t: (t, 0)),
                  pl.BlockSpec((tt, K), lambda t: (t, 0)),
                  pl.BlockSpec((None, D, F), lambda t: (layer, 0, 0)),
                  pl.BlockSpec((None, D, F), lambda t: (layer, 0, 0)),
                  pl.BlockSpec((None, F, D), lambda t: (layer, 0, 0)),
                  xrow,
                  pl.BlockSpec((None, 1, D), lambda t: ((layer * nb + t // nts) * N_MOD + which_gate, 0, 0)),
                  pl.BlockSpec(memory_space=pl.ANY)],
        out_specs=xrow,
        scratch_shapes=[pltpu.VMEM((K, tt, half), U32),
                        pltpu.VMEM((D, F), BF16), pltpu.VMEM((D, F), BF16), pltpu.VMEM((F, D), BF16),
                        pltpu.SemaphoreType.DMA(())],
        compiler_params=_params("arbitrary"),
        name="combine_shared_residual",
    )(dest, hp, wt, w_sg, w_su, w_sd, x, mod_rows, yp)


def _rmsnorm_kernel(x_ref, w_ref, out_ref):
    x = x_ref[...]
    out_ref[...] = x * lax.rsqrt(jnp.mean(x * x, axis=-1, keepdims=True) + EPS) * w_ref[...]


def rmsnorm(x, w):
    nb, S, D = x.shape
    ts = min(512, S)
    return pl.pallas_call(
        _rmsnorm_kernel,
        out_shape=jax.ShapeDtypeStruct((nb, S, D), F32),
        grid=(nb, S // ts),
        in_specs=[pl.BlockSpec((None, ts, D), lambda b, s: (b, s, 0)),
                  pl.BlockSpec((1, D), lambda b, s: (0, 0))],
        out_specs=pl.BlockSpec((None, ts, D), lambda b, s: (b, s, 0)),
        compiler_params=_params("parallel", "parallel"),
        name="final_rmsnorm",
    )(x, w.reshape(1, D))


def moe_block(x, norm_w, mod_rows, layer, w_router, router_bias, w_exp_gate, w_exp_up, w_exp_down,
              w_sh_gate, w_sh_up, w_sh_down):
    hp, logits = norm_router(x, norm_w, mod_rows, layer, w_router)
    eid, pos, wt, counts = route(logits, router_bias)
    counts = counts[:, 0]
    offsets = jnp.cumsum(counts) - counts
    dest = destination_rows(offsets, eid, pos)
    n_tok = hp.shape[0]
    token_of_row = (jnp.argsort(dest.reshape(-1)) % n_tok).astype(I32)[None, :]
    xp = dispatch_rows(hp, token_of_row)
    segments = expert_segments(counts, xp.shape[0], min(EXPERT_TILE, xp.shape[0]))
    yp = routed_experts(xp, segments, w_exp_gate, w_exp_up, w_exp_down, layer)
    return combine_shared_residual(yp, dest, wt.T, hp, w_sh_gate, w_sh_up, w_sh_down, x, mod_rows, layer, 5)


def kernel(x, c, norm_mix, norm_ffn, w_ada, b_ada, w_in, hgrn_lower_bounds, hgrn_out_norm, w_proj_a, w_proj_b, w_out, w_router, router_bias, w_exp_gate, w_exp_up, w_exp_down, w_sh_gate, w_sh_up, w_sh_down, norm_final):
    nb, S, D = x.shape
    depth = w_in.shape[0]
    a_width = w_proj_a.shape[1]
    b_width = w_proj_b.shape[1]
    in_cols = w_in.shape[2]
    qkv_cols = 3 * a_width
    rest_cols = in_cols - qkv_cols
    col_qb = 0
    col_fb = col_qb + b_width
    col_ib = col_fb + b_width
    col_gb = col_ib + b_width
    col_gate_a = col_gb + b_width
    col_gate_b = col_gate_a + D
    dilations = tuple(d for _, d in DILATED_PATTERNS)
    assert all(w // d == SPAN for w, d in DILATED_PATTERNS)

    lb_sm = jax.nn.softmax(hgrn_lower_bounds.astype(F32), axis=0)
    lb_all = jnp.cumsum(lb_sm, axis=0) - lb_sm[0:1]
    cos, sin = rope_tables(S)

    mod = adaln_modulation(c, w_ada, b_ada)
    mod_rows = mod.reshape(depth * nb * N_MOD, 1, D)

    for l in range(depth):
        h = norm_modulate(x, norm_mix[:, None, :], mod_rows, l, 0, 1)
        qkvs = qkv_projection(h, w_in, l, qkv_cols, 2 * a_width, cos, sin, dilations)
        rest = matmul_cols(h.reshape(nb * S, D), w_in, l, qkv_cols, rest_cols, BF16)

        outs, lses = [], []
        for qkv in qkvs:
            o_g, lse_g = dilated_attention_branch(qkv, a_width)
            outs.append(o_g)
            lses.append(jnp.transpose(lse_g, (0, 2, 1, 3)).reshape(nb, S, -1))
        o_a = attention_merge(outs, lses, dilations)

        o_b = hgrn2(rest.reshape(nb, S, rest_cols), col_qb, col_fb, col_ib, col_gb, b_width,
                    lb_all[l][None, :], hgrn_out_norm[l][None, :])

        u = merge_projection(o_a.reshape(nb * S, a_width), o_b.reshape(nb * S, b_width),
                             w_proj_a, w_proj_b, l, rest, col_gate_a, col_gate_b)
        x = projection_residual(u.reshape(nb, S, D), w_out, x, mod_rows, l, 2)

        x = moe_block(x, norm_ffn[:, None, :], mod_rows, l, w_router[l], router_bias[l],
                      w_exp_gate, w_exp_up, w_exp_down, w_sh_gate, w_sh_up, w_sh_down)
    return rmsnorm(x, norm_final)
```

```python
import functools

import jax
import jax.numpy as jnp
import numpy as np
from jax import lax
from jax.experimental import pallas as pl
from jax.experimental.pallas import tpu as pltpu

HEAD_DIM = 128
DILATED_PATTERNS = ((128, 1), (512, 4), (2048, 16))
SPAN = 128
ROPE_THETA = 10000.0
N_EXPERTS = 64
TOP_K = 8
N_GROUPS = 8
TOPK_GROUPS = 4
GROUP_SIZE = N_EXPERTS // N_GROUPS
ROUTED_SCALE = 2.5
N_MOD = 6
EPS = 1e-6

LANES = 128
SUBLANES = 8
VMEM_LIMIT_BYTES = 56 * 1024 * 1024

HGRN_CHUNK = 128
HGRN_LEVELS = 7

EXPERT_TILE = 512

BF16 = jnp.bfloat16
F32 = jnp.float32
U32 = jnp.uint32
I32 = jnp.int32

_NT = (((1,), (1,)), ((), ()))
_TN = (((0,), (0,)), ((), ()))


def _params(*sem):
    return pltpu.CompilerParams(dimension_semantics=sem, vmem_limit_bytes=VMEM_LIMIT_BYTES)


def _sigmoid(x):
    return 1.0 / (1.0 + jnp.exp(-x))


def _silu(x):
    return x * _sigmoid(x)


def _split_bf16(x):
    hi = x.astype(BF16)
    lo = (x - hi.astype(F32)).astype(BF16)
    return hi, lo


def _bf16_bits(x):
    return (lax.bitcast_convert_type(x, U32) + jnp.uint32(0x8000)) & jnp.uint32(0xFFFF0000)


def _pack_pair(lo, hi):
    return ((lax.bitcast_convert_type(lo, U32) + jnp.uint32(0x8000)) >> 16) | _bf16_bits(hi)


def _unpack_pair(p):
    lo = lax.bitcast_convert_type(p << 16, F32)
    hi = lax.bitcast_convert_type(p & jnp.uint32(0xFFFF0000), F32)
    return lo, hi


def _adaln_kernel(cb_ref, w_ref, bias_ref, out_ref, cs_ref, *, kc):
    K, bn = w_ref.shape
    nb = cb_ref.shape[0]

    @pl.when((pl.program_id(0) == 0) & (pl.program_id(1) == 0))
    def _():
        cs_ref[...] = _silu(cb_ref[...])

    for j in range(bn // LANES):
        cols = slice(j * LANES, (j + 1) * LANES)

        def body(i, accs):
            k0 = pl.multiple_of(i * kc, kc)
            w = w_ref[pl.ds(k0, kc), cols]
            out = []
            for b in range(nb):
                p = (w * cs_ref[b, pl.ds(k0, kc), :]).reshape(kc // SUBLANES, SUBLANES, LANES)
                out.append(accs[b] + jnp.sum(p, axis=0))
            return tuple(out)

        accs = lax.fori_loop(0, K // kc, body,
                             tuple(jnp.zeros((SUBLANES, LANES), F32) for _ in range(nb)))
        for b in range(nb):
            out_ref[b:b + 1, cols] = jnp.sum(accs[b], axis=0, keepdims=True) + bias_ref[:, cols]


def adaln_modulation(c, w_ada, b_ada):
    nl, K, N = w_ada.shape
    nb = c.shape[0]
    bn = min(512, N)
    kc = min(256, K)
    cb = jnp.broadcast_to(c[:, :, None], (nb, K, LANES))
    return pl.pallas_call(
        functools.partial(_adaln_kernel, kc=kc),
        out_shape=jax.ShapeDtypeStruct((nl, nb, N), F32),
        grid=(nl, N // bn),
        in_specs=[
            pl.BlockSpec((nb, K, LANES), lambda l, n: (0, 0, 0)),
            pl.BlockSpec((None, K, bn), lambda l, n: (l, 0, n)),
            pl.BlockSpec((None, 1, bn), lambda l, n: (l, 0, n)),
        ],
        out_specs=pl.BlockSpec((None, nb, bn), lambda l, n: (l, 0, n)),
        scratch_shapes=[pltpu.VMEM((nb, K, LANES), F32)],
        compiler_params=_params("arbitrary", "arbitrary"),
        name="adaln_modulation",
    )(cb, w_ada, b_ada.reshape(nl, 1, N))


def _norm_mod(x, w, shift, scale):
    y = x * lax.rsqrt(jnp.mean(x * x, axis=-1, keepdims=True) + EPS) * w
    return y * (1.0 + scale) + shift


def _norm_mod_kernel(x_ref, w_ref, shift_ref, scale_ref, out_ref):
    out_ref[...] = _norm_mod(x_ref[...], w_ref[...], shift_ref[...], scale_ref[...]).astype(out_ref.dtype)


def _mod_spec(D, which, layer, nb):
    return pl.BlockSpec((None, 1, D), lambda b, *_: ((layer * nb + b) * N_MOD + which, 0, 0))


def norm_modulate(x, norm_w, mod_rows, layer, which_shift, which_scale):
    nb, S, D = x.shape
    ts = min(512, S)
    return pl.pallas_call(
        _norm_mod_kernel,
        out_shape=jax.ShapeDtypeStruct((nb, S, D), BF16),
        grid=(nb, S // ts),
        in_specs=[
            pl.BlockSpec((None, ts, D), lambda b, s: (b, s, 0)),
            pl.BlockSpec((None, 1, D), lambda b, s: (layer, 0, 0)),
            _mod_spec(D, which_shift, layer, nb),
            _mod_spec(D, which_scale, layer, nb),
        ],
        out_specs=pl.BlockSpec((None, ts, D), lambda b, s: (b, s, 0)),
        compiler_params=_params("parallel", "parallel"),
        name="norm_modulate",
    )(x, norm_w, mod_rows, mod_rows)


def _qkv_proj_kernel(x_ref, w_ref, cos_ref, sin_ref, *refs, n_rope_tiles, dilations):
    out_refs, scr = refs[:-1], refs[-1]
    n_heads, bm, _ = scr.shape
    y = jnp.dot(x_ref[...], w_ref[...].astype(BF16), preferred_element_type=F32)
    n = pl.program_id(2)

    @pl.when(n < n_rope_tiles)
    def _():
        cos = cos_ref[...]
        sin = sin_ref[...]
        for h in range(n_heads):
            t = y[:, h * HEAD_DIM:(h + 1) * HEAD_DIM]
            scr[h] = t * cos + pltpu.roll(t, HEAD_DIM // 2, 1) * sin

    @pl.when(n >= n_rope_tiles)
    def _():
        for h in range(n_heads):
            scr[h] = y[:, h * HEAD_DIM:(h + 1) * HEAD_DIM]

    for d, o_ref in zip(dilations, out_refs):
        for r in range(d):
            for h in range(n_heads):
                o_ref[r, :, h * HEAD_DIM:(h + 1) * HEAD_DIM] = (
                    scr[h, pl.ds(r, bm // d, stride=d), :].astype(o_ref.dtype))


def rope_tables(S):
    half = HEAD_DIM // 2
    inv = ROPE_THETA ** (-jnp.arange(half, dtype=F32) / half)
    ang = jnp.arange(S, dtype=F32)[:, None] * inv[None, :]
    cos, sin = jnp.cos(ang), jnp.sin(ang)
    return jnp.concatenate([cos, cos], axis=-1), jnp.concatenate([-sin, sin], axis=-1)


def qkv_projection(h, w_in, layer, qkv_cols, rope_cols, cos, sin, dilations):
    nb, S, D = h.shape
    bm, bn = min(1024, S), min(512, qkv_cols)
    outs = tuple(jax.ShapeDtypeStruct((nb, d, S // d, qkv_cols), BF16) for d in dilations)
    out_specs = tuple(pl.BlockSpec((None, d, bm // d, bn), lambda b, m, n: (b, 0, m, n)) for d in dilations)
    return pl.pallas_call(
        functools.partial(_qkv_proj_kernel, n_rope_tiles=rope_cols // bn, dilations=dilations),
        out_shape=outs,
        grid=(nb, S // bm, qkv_cols // bn),
        in_specs=[pl.BlockSpec((None, bm, D), lambda b, m, n: (b, m, 0)),
                  pl.BlockSpec((None, D, bn), lambda b, m, n: (layer, 0, n)),
                  pl.BlockSpec((bm, HEAD_DIM), lambda b, m, n: (m, 0)),
                  pl.BlockSpec((bm, HEAD_DIM), lambda b, m, n: (m, 0))],
        out_specs=out_specs,
        scratch_shapes=[pltpu.VMEM((bn // HEAD_DIM, bm, HEAD_DIM), F32)],
        compiler_params=_params("parallel", "parallel", "arbitrary"),
        name="qkv_projection",
    )(h, w_in, cos, sin)


def _matmul_kernel(x_ref, w_ref, out_ref):
    out_ref[...] = jnp.dot(x_ref[...], w_ref[...].astype(BF16),
                           preferred_element_type=F32).astype(out_ref.dtype)


def matmul_cols(x, w, layer, col0, ncols, out_dtype):
    M, K = x.shape
    bm, bn = min(1024, M), min(512, ncols)
    return pl.pallas_call(
        _matmul_kernel,
        out_shape=jax.ShapeDtypeStruct((M, ncols), out_dtype),
        grid=(M // bm, ncols // bn),
        in_specs=[pl.BlockSpec((bm, K), lambda m, n: (m, 0)),
                  pl.BlockSpec((None, K, bn), lambda m, n: (layer, 0, col0 // bn + n))],
        out_specs=pl.BlockSpec((bm, bn), lambda m, n: (m, n)),
        compiler_params=_params("parallel", "arbitrary"),
        name="matmul_cols",
    )(x, w)


def _attn_kernel(q_ref, kp_ref, kc_ref, vp_ref, vc_ref, o_ref, lse_ref):
    i = pl.program_id(2)
    n_heads = q_ref.shape[-1] // HEAD_DIM
    qi = lax.broadcasted_iota(I32, (SPAN, 2 * SPAN), 0)
    kj = lax.broadcasted_iota(I32, (SPAN, 2 * SPAN), 1)
    mask = ((kj < SPAN) & (kj >= qi) & (i > 0)) | ((kj >= SPAN) & ((kj - SPAN) <= qi))
    scale = HEAD_DIM ** -0.5
    s = []
    for h in range(n_heads):
        cols = slice(h * HEAD_DIM, (h + 1) * HEAD_DIM)
        k_h = jnp.concatenate([kp_ref[:, cols], kc_ref[:, cols]], axis=0)
        s.append(lax.dot_general(q_ref[:, cols], k_h, _NT, preferred_element_type=F32))
    s = jnp.stack(s, axis=0)
    s = jnp.where(mask[None], s * scale, -jnp.inf)
    m = jnp.max(s, axis=-1, keepdims=True)
    p = jnp.exp(s - m).astype(BF16)
    ones = jnp.ones((2 * SPAN, HEAD_DIM), BF16)
    lses = []
    for h in range(n_heads):
        cols = slice(h * HEAD_DIM, (h + 1) * HEAD_DIM)
        v_h = jnp.concatenate([vp_ref[:, cols], vc_ref[:, cols]], axis=0)
        oe = jnp.dot(p[h], jnp.concatenate([v_h, ones], axis=1), preferred_element_type=F32)
        den = oe[:, HEAD_DIM:]
        o_ref[:, cols] = (oe[:, :HEAD_DIM] / den).astype(o_ref.dtype)
        lses.append(m[h] + jnp.log(den[:, :1]))
    lse_ref[...] = jnp.concatenate(lses, axis=-1)


def dilated_attention_branch(qkv, width):
    nb, d, L, _ = qkv.shape
    nblk = L // SPAN
    n_heads = width // HEAD_DIM

    def spec(col, prev):
        if prev:
            return pl.BlockSpec((None, None, SPAN, width), lambda b, r, i: (b, r, jnp.maximum(i - 1, 0), col))
        return pl.BlockSpec((None, None, SPAN, width), lambda b, r, i: (b, r, i, col))

    return pl.pallas_call(
        _attn_kernel,
        out_shape=(jax.ShapeDtypeStruct((nb, d, L, width), BF16),
                   jax.ShapeDtypeStruct((nb, d, L, n_heads), F32)),
        grid=(nb, d, nblk),
        in_specs=[spec(0, False), spec(1, True), spec(1, False), spec(2, True), spec(2, False)],
        out_specs=(pl.BlockSpec((None, None, SPAN, width), lambda b, r, i: (b, r, i, 0)),
                   pl.BlockSpec((None, None, SPAN, n_heads), lambda b, r, i: (b, r, i, 0))),
        compiler_params=_params("parallel", "parallel", "arbitrary"),
        name=f"dilated_attention_d{d}",
    )(qkv, qkv, qkv, qkv, qkv)


def _attn_merge_kernel(*refs, dilations):
    n = len(dilations)
    o_refs, lse_refs, out_ref, scr_refs = refs[:n], refs[n:2 * n], refs[2 * n], refs[2 * n + 1:]
    ts = out_ref.shape[0]
    nat = []
    for g, d in enumerate(dilations):
        if d == 1:
            nat.append(None)
            continue
        scr = scr_refs[len([x for x in nat if x is not None])]
        for r in range(d):
            for h in range(scr.shape[0]):
                scr[h, pl.ds(r, ts // d, stride=d), :] = (
                    o_refs[g][r, :, h * HEAD_DIM:(h + 1) * HEAD_DIM].astype(F32))
        nat.append(scr)
    lses = [r[...] for r in lse_refs]
    m = functools.reduce(jnp.maximum, lses)
    es = [jnp.exp(l - m) for l in lses]
    tot = functools.reduce(jnp.add, es)
    ws = [e / tot for e in es]
    for h in range(out_ref.shape[-1] // HEAD_DIM):
        cols = slice(h * HEAD_DIM, (h + 1) * HEAD_DIM)
        acc = jnp.zeros((ts, HEAD_DIM), F32)
        for g in range(n):
            o = o_refs[g][0, :, cols].astype(F32) if nat[g] is None else nat[g][h]
            acc = acc + ws[g][:, h:h + 1] * o
        out_ref[:, cols] = acc.astype(out_ref.dtype)


def attention_merge(outs, lses, dilations):
    nb, _, S, width = outs[dilations.index(1)].shape
    n_heads = lses[0].shape[-1]
    ts = min(512, S)
    o_specs = [pl.BlockSpec((None, d, ts // d, width), lambda b, s: (b, 0, s, 0)) for d in dilations]
    l_spec = pl.BlockSpec((None, ts, n_heads), lambda b, s: (b, s, 0))
    return pl.pallas_call(
        functools.partial(_attn_merge_kernel, dilations=dilations),
        out_shape=jax.ShapeDtypeStruct((nb, S, width), BF16),
        grid=(nb, S // ts),
        in_specs=o_specs + [l_spec] * len(lses),
        out_specs=pl.BlockSpec((None, ts, width), lambda b, s: (b, s, 0)),
        scratch_shapes=[pltpu.VMEM((n_heads, ts, HEAD_DIM), F32) for d in dilations if d > 1],
        compiler_params=_params("parallel", "parallel"),
        name="attention_merge",
    )(*outs, *lses)


def _hgrn_sum_matrix(C, levels):
    t = np.arange(C)[:, None]
    u = np.arange(C)[None, :]
    blocks = [(u <= t), (u > t)]
    for j in range(levels):
        half = C >> (j + 1)
        mid = (t // (2 * half)) * (2 * half) + half - 1
        upper = (t // half) % 2 == 1
        blocks.append(np.where(upper, (u > mid) & (u <= t), (u > t) & (u <= mid)))
    return np.concatenate(blocks, axis=0).astype(np.float32)


def _hgrn_kernel(q_ref, f_ref, i_ref, g_ref, lb_ref, nw_ref, sm_ref, out_ref, state_ref, *, n_chunks):
    C = HGRN_CHUNK

    @pl.when(pl.program_id(2) == 0)
    def _():
        state_ref[...] = jnp.zeros_like(state_ref)

    ti = lax.broadcasted_iota(I32, (C, C), 0)
    si = lax.broadcasted_iota(I32, (C, C), 1)
    xor = ti ^ si
    lower = si < ti
    lb = lb_ref[...]
    nw = nw_ref[...]
    ones = jnp.ones((C, LANES), BF16)
    state = state_ref[...]

    for c in range(n_chunks):
        rows = slice(c * C, (c + 1) * C)
        q = _silu(q_ref[rows, :].astype(F32))
        f = lb + (1.0 - lb) * _sigmoid(f_ref[rows, :].astype(F32))
        k = 1.0 - f
        g = jnp.log(f)
        v = i_ref[rows, :]
        g_hi, g_lo = _split_bf16(g)
        ghl = jnp.concatenate([g_hi, g_lo], axis=-1)
        e2 = jnp.dot(sm_ref[...], ghl, preferred_element_type=F32)
        e = e2[:, :HEAD_DIM] + e2[:, HEAD_DIM:]
        b_end = (lax.dot_general(g_hi, ones, _TN, preferred_element_type=F32)
                 + lax.dot_general(g_lo, ones, _TN, preferred_element_type=F32))

        scores = jnp.where(ti == si, lax.dot_general(q.astype(BF16), k.astype(BF16), _NT,
                                                     preferred_element_type=F32), 0.0)
        for j in range(HGRN_LEVELS):
            a = jnp.exp(e[(2 + j) * C:(3 + j) * C, :])
            s_j = lax.dot_general((q * a).astype(BF16), (k * a).astype(BF16), _NT,
                                  preferred_element_type=F32)
            scores = scores + jnp.where(lower & ((xor >> (HGRN_LEVELS - 1 - j)) == 1), s_j, 0.0)

        o = jnp.dot(scores.astype(BF16), v, preferred_element_type=F32)
        o = o + jnp.dot((q * jnp.exp(e[0:C, :])).astype(BF16), state.astype(BF16),
                        preferred_element_type=F32)
        k_end = (k * jnp.exp(e[C:2 * C, :])).astype(BF16)
        state = jnp.exp(b_end) * state + lax.dot_general(k_end, v, _TN, preferred_element_type=F32)

        y = o * lax.rsqrt(jnp.mean(o * o, axis=-1, keepdims=True) + EPS) * nw
        out_ref[rows, :] = (y * _silu(g_ref[rows, :].astype(F32))).astype(out_ref.dtype)
    state_ref[...] = state


def hgrn2(proj, col_q, col_f, col_i, col_g, width, lb, out_norm):
    nb, S, _ = proj.shape
    n_heads = width // HEAD_DIM
    C = HGRN_CHUNK
    ts = min(512, S)
    sm = jnp.asarray(_hgrn_sum_matrix(C, HGRN_LEVELS), dtype=BF16)

    def col_spec(col0):
        blk = col0 // HEAD_DIM
        return pl.BlockSpec((None, ts, HEAD_DIM), lambda b, h, s: (b, s, blk + h))

    return pl.pallas_call(
        functools.partial(_hgrn_kernel, n_chunks=ts // C),
        out_shape=jax.ShapeDtypeStruct((nb, S, width), BF16),
        grid=(nb, n_heads, S // ts),
        in_specs=[col_spec(col_q), col_spec(col_f), col_spec(col_i), col_spec(col_g),
                  pl.BlockSpec((1, HEAD_DIM), lambda b, h, s: (0, h)),
                  pl.BlockSpec((1, HEAD_DIM), lambda b, h, s: (0, 0)),
                  pl.BlockSpec(sm.shape, lambda b, h, s: (0, 0))],
        out_specs=pl.BlockSpec((None, ts, HEAD_DIM), lambda b, h, s: (b, s, h)),
        scratch_shapes=[pltpu.VMEM((HEAD_DIM, HEAD_DIM), F32)],
        compiler_params=_params("parallel", "parallel", "arbitrary"),
        name="hgrn2",
    )(proj, proj, proj, proj, lb, out_norm, sm)


def _merge_proj_kernel(oa_ref, ob_ref, wa_ref, wb_ref, ga_ref, gb_ref, out_ref):
    ya = jnp.dot(oa_ref[...], wa_ref[...].astype(BF16), preferred_element_type=F32)
    yb = jnp.dot(ob_ref[...], wb_ref[...].astype(BF16), preferred_element_type=F32)
    u = _sigmoid(ga_ref[...].astype(F32)) * ya + _sigmoid(gb_ref[...].astype(F32)) * yb
    out_ref[...] = u.astype(out_ref.dtype)


def merge_projection(o_a, o_b, w_a, w_b, layer, proj, col_ga, col_gb):
    M, K = o_a.shape
    N = w_a.shape[2]
    bm, bn = min(1024, M), min(512, N)
    return pl.pallas_call(
        _merge_proj_kernel,
        out_shape=jax.ShapeDtypeStruct((M, N), BF16),
        grid=(M // bm, N // bn),
        in_specs=[pl.BlockSpec((bm, K), lambda m, n: (m, 0)),
                  pl.BlockSpec((bm, K), lambda m, n: (m, 0)),
                  pl.BlockSpec((None, K, bn), lambda m, n: (layer, 0, n)),
                  pl.BlockSpec((None, K, bn), lambda m, n: (layer, 0, n)),
                  pl.BlockSpec((bm, bn), lambda m, n: (m, col_ga // bn + n)),
                  pl.BlockSpec((bm, bn), lambda m, n: (m, col_gb // bn + n))],
        out_specs=pl.BlockSpec((bm, bn), lambda m, n: (m, n)),
        compiler_params=_params("parallel", "arbitrary"),
        name="merge_projection",
    )(o_a, o_b, w_a, w_b, proj, proj)


def _proj_residual_kernel(u_ref, w_ref, x_ref, gate_ref, out_ref):
    y = jnp.dot(u_ref[...], w_ref[...].astype(BF16), preferred_element_type=F32)
    out_ref[...] = x_ref[...] + gate_ref[...] * y


def projection_residual(u, w, x, mod_rows, layer, which_gate):
    nb, S, K = u.shape
    D = w.shape[2]
    bm, bn = min(1024, S), min(512, D)
    return pl.pallas_call(
        _proj_residual_kernel,
        out_shape=jax.ShapeDtypeStruct((nb, S, D), F32),
        grid=(nb, S // bm, D // bn),
        in_specs=[pl.BlockSpec((None, bm, K), lambda b, m, n: (b, m, 0)),
                  pl.BlockSpec((None, K, bn), lambda b, m, n: (layer, 0, n)),
                  pl.BlockSpec((None, bm, bn), lambda b, m, n: (b, m, n)),
                  pl.BlockSpec((None, 1, bn), lambda b, m, n: ((layer * nb + b) * N_MOD + which_gate, 0, n))],
        out_specs=pl.BlockSpec((None, bm, bn), lambda b, m, n: (b, m, n)),
        compiler_params=_params("parallel", "parallel", "arbitrary"),
        name="projection_residual",
    )(u, w, x, mod_rows)


def _norm_router_kernel(x_ref, w_ref, shift_ref, scale_ref, wr_hi_ref, wr_lo_ref, hp_ref, logit_ref):
    h = _norm_mod(x_ref[...], w_ref[...], shift_ref[...], scale_ref[...])
    half = h.shape[1] // 2
    bits = _bf16_bits(h)
    hp_ref[...] = (bits[:, :half] >> 16) | bits[:, half:]
    h_r = lax.bitcast_convert_type(bits, F32)
    h_hi = h_r.astype(BF16)
    h_lo = (h - h_r).astype(BF16)
    logit_ref[...] = (lax.dot_general(wr_hi_ref[...], h_hi, _NT, preferred_element_type=F32)
                      + lax.dot_general(wr_hi_ref[...], h_lo, _NT, preferred_element_type=F32)
                      + lax.dot_general(wr_lo_ref[...], h_hi, _NT, preferred_element_type=F32))


def norm_router(x, norm_w, mod_rows, layer, w_router):
    nb, S, D = x.shape
    E = w_router.shape[1]
    ts = min(512, S)
    nts = S // ts
    wr_hi, wr_lo = _split_bf16(w_router.T)
    return pl.pallas_call(
        _norm_router_kernel,
        out_shape=(jax.ShapeDtypeStruct((nb * S, D // 2), U32), jax.ShapeDtypeStruct((E, nb * S), F32)),
        grid=(nb, nts),
        in_specs=[pl.BlockSpec((None, ts, D), lambda b, s: (b, s, 0)),
                  pl.BlockSpec((None, 1, D), lambda b, s: (layer, 0, 0)),
                  _mod_spec(D, 3, layer, nb),
                  _mod_spec(D, 4, layer, nb),
                  pl.BlockSpec((E, D), lambda b, s: (0, 0)),
                  pl.BlockSpec((E, D), lambda b, s: (0, 0))],
        out_specs=(pl.BlockSpec((ts, D // 2), lambda b, s: (b * nts + s, 0)),
                   pl.BlockSpec((E, ts), lambda b, s: (0, b * nts + s))),
        compiler_params=_params("parallel", "parallel"),
        name="norm_router",
    )(x, norm_w, mod_rows, mod_rows, wr_hi, wr_lo)


def _rank_lt(vals, n_rows, limit):
    ridx = lax.broadcasted_iota(I32, vals.shape, 0)
    cnt = jnp.zeros(vals.shape, I32)
    for r in range(n_rows):
        other = vals[r:r + 1, :]
        beats = (other > vals) | ((other == vals) & (r < ridx))
        cnt = cnt + beats.astype(I32)
    return cnt < limit


def _route_kernel(logit_ref, bias_ref, trie_ref, trit_ref, eid_ref, pos_ref, wt_ref, cnt_ref, carry_ref):
    @pl.when(pl.program_id(0) == 0)
    def _():
        carry_ref[...] = jnp.zeros_like(carry_ref)

    scores = _sigmoid(logit_ref[...])
    sel = scores + bias_ref[...]
    E, T = sel.shape
    sub = lax.broadcasted_iota(I32, (GROUP_SIZE, T), 0)
    gscores = []
    for g in range(N_GROUPS):
        v = sel[g * GROUP_SIZE:(g + 1) * GROUP_SIZE, :]
        m1 = jnp.max(v, axis=0, keepdims=True)
        first = jnp.min(jnp.where(v == m1, sub, GROUP_SIZE), axis=0, keepdims=True)
        m2 = jnp.max(jnp.where(sub == first, -jnp.inf, v), axis=0, keepdims=True)
        gscores.append(m1 + m2)
    gsc = jnp.concatenate(gscores, axis=0)
    gkeep = _rank_lt(gsc, N_GROUPS, TOPK_GROUPS)
    ekeep = jnp.concatenate(
        [jnp.broadcast_to(gkeep[g:g + 1, :], (GROUP_SIZE, T)) for g in range(N_GROUPS)], axis=0)
    masked = jnp.where(ekeep, sel, -jnp.inf)
    chosen = _rank_lt(masked, N_EXPERTS, TOP_K)
    w = jnp.where(chosen, scores, 0.0)
    combine = w / jnp.sum(w, axis=0, keepdims=True) * ROUTED_SCALE

    cf = jnp.where(chosen, 1.0, 0.0)
    cb = cf.astype(BF16)
    rank = jnp.dot(trie_ref[...], cb, preferred_element_type=F32)
    local = jnp.dot(cb, trit_ref[...], preferred_element_type=F32)
    carry = carry_ref[:, :1]
    pos = carry + local
    eidx = lax.broadcasted_iota(I32, (E, T), 0).astype(F32)
    eids, poss, wts = [], [], []
    for k in range(TOP_K):
        sel_k = chosen & (rank == k)
        eids.append(jnp.sum(jnp.where(sel_k, eidx, 0.0), axis=0, keepdims=True))
        poss.append(jnp.sum(jnp.where(sel_k, pos, 0.0), axis=0, keepdims=True))
        wts.append(jnp.sum(jnp.where(sel_k, combine, 0.0), axis=0, keepdims=True))
    eid_ref[...] = jnp.concatenate(eids, axis=0).astype(I32)
    pos_ref[...] = jnp.concatenate(poss, axis=0).astype(I32)
    wt_ref[...] = jnp.concatenate(wts, axis=0)
    total = carry + jnp.sum(cf, axis=1, keepdims=True)
    carry_ref[...] = jnp.broadcast_to(total, carry_ref.shape)
    cnt_ref[...] = jnp.broadcast_to(total, cnt_ref.shape).astype(I32)


def route(logits, router_bias):
    E, T = logits.shape
    tt = min(512, T)
    trie = jnp.asarray(np.tril(np.ones((E, E), np.float32), -1), BF16)
    trit = jnp.asarray(np.triu(np.ones((tt, tt), np.float32), 1), BF16)
    kt = pl.BlockSpec((TOP_K, tt), lambda t: (0, t))
    return pl.pallas_call(
        _route_kernel,
        out_shape=(jax.ShapeDtypeStruct((TOP_K, T), I32), jax.ShapeDtypeStruct((TOP_K, T), I32),
                   jax.ShapeDtypeStruct((TOP_K, T), F32), jax.ShapeDtypeStruct((E, LANES), I32)),
        grid=(T // tt,),
        in_specs=[pl.BlockSpec((E, tt), lambda t: (0, t)),
                  pl.BlockSpec((E, 1), lambda t: (0, 0)),
                  pl.BlockSpec((E, E), lambda t: (0, 0)),
                  pl.BlockSpec((tt, tt), lambda t: (0, 0))],
        out_specs=(kt, kt, kt, pl.BlockSpec((E, LANES), lambda t: (0, 0))),
        scratch_shapes=[pltpu.VMEM((E, LANES), F32)],
        compiler_params=_params("arbitrary"),
        name="route",
    )(logits, router_bias.reshape(E, 1), trie, trit)


def _dest_kernel(off_ref, eid_ref, pos_ref, dest_ref):
    eid = eid_ref[...]
    acc = pos_ref[...]
    for e in range(N_EXPERTS):
        acc = acc + jnp.where(eid == e, off_ref[e], 0)
    dest_ref[...] = acc


def destination_rows(offsets, eid, pos):
    K, T = eid.shape
    tt = min(2048, T)
    kt = pl.BlockSpec((K, tt), lambda t, off: (0, t))
    return pl.pallas_call(
        _dest_kernel,
        out_shape=jax.ShapeDtypeStruct((K, T), I32),
        grid_spec=pltpu.PrefetchScalarGridSpec(num_scalar_prefetch=1, grid=(T // tt,),
                                               in_specs=[kt, kt], out_specs=kt),
        compiler_params=_params("parallel"),
        name="destination_rows",
    )(offsets, eid, pos)


def _new_expert(eid_ref):
    v = pl.program_id(0)
    return (v == 0) | (eid_ref[v] != eid_ref[jnp.maximum(v - 1, 0)])


def _write_rows(out_ref, vals, tile_ref, lo_ref, hi_ref, first_ref):
    v = pl.program_id(0)
    tm = out_ref.shape[0]
    row = tile_ref[v] * tm + lax.broadcasted_iota(I32, (tm, 1), 0)
    mine = (row >= lo_ref[v]) & (row < hi_ref[v])

    @pl.when(first_ref[v] == 1)
    def _():
        out_ref[...] = jnp.where(mine, vals, jnp.zeros_like(vals))

    @pl.when(first_ref[v] == 0)
    def _():
        out_ref[...] = jnp.where(mine, vals, out_ref[...])


def _experts_up_kernel(tile_ref, eid_ref, lo_ref, hi_ref, first_ref, tok_ref, nxt_ref, hp_hbm, wg_ref, wu_ref,
                       act_ref, xbuf, wg_s, wu_s, sem, *, n_tiles):
    @pl.when(_new_expert(eid_ref))
    def _():
        wg_s[...] = wg_ref[...].astype(BF16)
        wu_s[...] = wu_ref[...].astype(BF16)

    v = pl.program_id(0)
    tm, half = xbuf.shape[1:]
    tile = tile_ref[v]
    slot = tile % 2

    def row_copy(idx_ref, i, s):
        return pltpu.make_async_copy(hp_hbm.at[pl.ds(idx_ref[0, i], 1)], xbuf.at[s, pl.ds(i, 1)], sem.at[s])

    def gate_up(x):
        x_lo, x_hi = _unpack_pair(x)
        x_lo, x_hi = x_lo.astype(BF16), x_hi.astype(BF16)
        gate = (jnp.dot(x_lo, wg_s[:half, :], preferred_element_type=F32)
                + jnp.dot(x_hi, wg_s[half:, :], preferred_element_type=F32))
        up = (jnp.dot(x_lo, wu_s[:half, :], preferred_element_type=F32)
              + jnp.dot(x_hi, wu_s[half:, :], preferred_element_type=F32))
        return _silu(gate) * up

    nonempty = hi_ref[v] > lo_ref[v]
    row = tile * tm + lax.broadcasted_iota(I32, (tm, 1), 0)
    mine = (row >= lo_ref[v]) & (row < hi_ref[v])

    def first_visit(request_next):
        @pl.when(tile == 0)
        def _():
            def start(i, carry):
                row_copy(tok_ref, i, 0).start()
                return carry
            lax.fori_loop(0, tm, start, 0, unroll=8)

        def wait(i, carry):
            row_copy(tok_ref, i, slot).wait()
            return carry
        lax.fori_loop(0, tm, wait, 0, unroll=8)

        vals = gate_up(xbuf[slot])
        if request_next:
            for i in range(tm):
                row_copy(nxt_ref, i, 1 - slot).start()
        act_ref[...] = jnp.where(mine, vals, 0.0)

    is_first = nonempty & (first_ref[v] == 1)

    @pl.when(is_first & (tile + 1 < n_tiles))
    def _():
        first_visit(True)

    @pl.when(is_first & (tile + 1 >= n_tiles))
    def _():
        first_visit(False)

    @pl.when(nonempty & (first_ref[v] == 0))
    def _():
        vals = gate_up(xbuf[slot])
        act_ref[...] = jnp.where(mine, vals, act_ref[...])


def _experts_down_kernel(tile_ref, eid_ref, lo_ref, hi_ref, first_ref, act_ref, wd_ref, y_ref, wd_s):
    @pl.when(_new_expert(eid_ref))
    def _():
        wd_s[...] = wd_ref[...].astype(BF16)

    v = pl.program_id(0)

    @pl.when(hi_ref[v] > lo_ref[v])
    def _():
        half = y_ref.shape[1]
        act = act_ref[...].astype(BF16)
        packed = _pack_pair(jnp.dot(act, wd_s[:, :half], preferred_element_type=F32),
                            jnp.dot(act, wd_s[:, half:], preferred_element_type=F32))
        _write_rows(y_ref, packed, tile_ref, lo_ref, hi_ref, first_ref)


def expert_segments(counts, n_rows, tm):
    E = counts.shape[0]
    n_tiles = n_rows // tm
    ends = jnp.cumsum(counts)
    starts = ends - counts
    cuts = jnp.sort(jnp.concatenate([jnp.arange(n_tiles, dtype=I32) * tm, starts.astype(I32)]))
    lo = cuts
    hi = jnp.concatenate([cuts[1:], jnp.array([n_rows], I32)])
    tile = jnp.minimum(lo // tm, n_tiles - 1)
    eid = jnp.minimum(jnp.sum(ends[None, :] <= lo[:, None], axis=1), E - 1).astype(I32)
    first = ((lo % tm == 0) & (hi > lo)).astype(I32)
    return tile.astype(I32), eid, lo.astype(I32), hi.astype(I32), first


def routed_experts(hp, token_of_row, segments, w_gate, w_up, w_down, layer):
    _, half = hp.shape
    R = token_of_row.shape[1]
    _, E, D, F = w_gate.shape
    tm = min(EXPERT_TILE, R)
    n_tiles = R // tm
    n_visits = segments[0].shape[0]

    def tile_spec(width):
        return pl.BlockSpec((tm, width), lambda v, tile, eid, lo, hi, first: (tile[v], 0))

    def weight_spec(rows, cols):
        return pl.BlockSpec((None, None, rows, cols), lambda v, tile, eid, lo, hi, first: (layer, eid[v], 0, 0))

    act = pl.pallas_call(
        functools.partial(_experts_up_kernel, n_tiles=n_tiles),
        out_shape=jax.ShapeDtypeStruct((R, F), F32),
        grid_spec=pltpu.PrefetchScalarGridSpec(
            num_scalar_prefetch=5, grid=(n_visits,),
            in_specs=[pl.BlockSpec((1, tm), lambda v, tile, eid, lo, hi, first: (0, tile[v]),
                                   memory_space=pltpu.SMEM),
                      pl.BlockSpec((1, tm), lambda v, tile, eid, lo, hi, first:
                                   (0, jnp.minimum(tile[v] + 1, n_tiles - 1)), memory_space=pltpu.SMEM),
                      pl.BlockSpec(memory_space=pl.ANY),
                      weight_spec(D, F), weight_spec(D, F)],
            out_specs=tile_spec(F),
            scratch_shapes=[pltpu.VMEM((2, tm, half), U32),
                            pltpu.VMEM((D, F), BF16), pltpu.VMEM((D, F), BF16),
                            pltpu.SemaphoreType.DMA((2,))]),
        compiler_params=_params("arbitrary"),
        name="routed_experts_up",
    )(*segments, token_of_row, token_of_row, hp, w_gate, w_up)
    return pl.pallas_call(
        _experts_down_kernel,
        out_shape=jax.ShapeDtypeStruct((R, half), U32),
        grid_spec=pltpu.PrefetchScalarGridSpec(
            num_scalar_prefetch=5, grid=(n_visits,),
            in_specs=[tile_spec(F), weight_spec(F, D)],
            out_specs=tile_spec(half),
            scratch_shapes=[pltpu.VMEM((F, D), BF16)]),
        compiler_params=_params("arbitrary"),
        name="routed_experts_down",
    )(*segments, act, w_down)


def _combine_kernel(dest_ref, hp_ref, wt_ref, sg_ref, su_ref, sd_ref, x_ref, gate_ref, yp_hbm, out_ref,
                    buf, sg_s, su_s, sd_s, sem, *, tt):
    @pl.when(pl.program_id(0) == 0)
    def _():
        sg_s[...] = sg_ref[...].astype(BF16)
        su_s[...] = su_ref[...].astype(BF16)
        sd_s[...] = sd_ref[...].astype(BF16)

    def row_copy(i, k):
        return pltpu.make_async_copy(yp_hbm.at[pl.ds(dest_ref[k, i], 1)], buf.at[k, pl.ds(i, 1)], sem)

    def start(i, carry):
        for k in range(TOP_K):
            row_copy(i, k).start()
        return carry

    def wait(i, carry):
        for k in range(TOP_K):
            row_copy(i, k).wait()
        return carry

    lax.fori_loop(0, tt, start, 0)

    half = hp_ref.shape[1]
    h_lo, h_hi = _unpack_pair(hp_ref[...])
    h_lo, h_hi = h_lo.astype(BF16), h_hi.astype(BF16)
    g = (jnp.dot(h_lo, sg_s[:half, :], preferred_element_type=F32)
         + jnp.dot(h_hi, sg_s[half:, :], preferred_element_type=F32))
    u = (jnp.dot(h_lo, su_s[:half, :], preferred_element_type=F32)
         + jnp.dot(h_hi, su_s[half:, :], preferred_element_type=F32))
    act = (_silu(g) * u).astype(BF16)
    y_lo = jnp.dot(act, sd_s[:, :half], preferred_element_type=F32)
    y_hi = jnp.dot(act, sd_s[:, half:], preferred_element_type=F32)

    lax.fori_loop(0, tt, wait, 0)

    for k in range(TOP_K):
        e_lo, e_hi = _unpack_pair(buf[k])
        wk = wt_ref[:, k:k + 1]
        y_lo = y_lo + wk * e_lo
        y_hi = y_hi + wk * e_hi
    gate = gate_ref[...]
    out_ref[:, :half] = x_ref[:, :half] + gate[:, :half] * y_lo
    out_ref[:, half:] = x_ref[:, half:] + gate[:, half:] * y_hi


def combine_shared_residual(yp, dest, wt, hp, w_sg, w_su, w_sd, x, mod_rows, layer, which_gate):
    nb, S, D = x.shape
    T, half = hp.shape
    K = dest.shape[0]
    F = w_sg.shape[2]
    tt = min(128, S)
    nts = S // tt
    xrow = pl.BlockSpec((None, tt, D), lambda t: (t // nts, t % nts, 0))
    return pl.pallas_call(
        functools.partial(_combine_kernel, tt=tt),
        out_shape=jax.ShapeDtypeStruct((nb, S, D), F32),
        grid=(T // tt,),
        in_specs=[pl.BlockSpec((K, tt), lambda t: (0, t), memory_space=pltpu.SMEM),
                  pl.BlockSpec((tt, half), lambda t: (t, 0)),
                  pl.BlockSpec((tt, K), lambda t: (t, 0)),
                  pl.BlockSpec((None, D, F), lambda t: (layer, 0, 0)),
                  pl.BlockSpec((None, D, F), lambda t: (layer, 0, 0)),
                  pl.BlockSpec((None, F, D), lambda t: (layer, 0, 0)),
                  xrow,
                  pl.BlockSpec((None, 1, D), lambda t: ((layer * nb + t // nts) * N_MOD + which_gate, 0, 0)),
                  pl.BlockSpec(memory_space=pl.ANY)],
        out_specs=xrow,
        scratch_shapes=[pltpu.VMEM((K, tt, half), U32),
                        pltpu.VMEM((D, F), BF16), pltpu.VMEM((D, F), BF16), pltpu.VMEM((F, D), BF16),
                        pltpu.SemaphoreType.DMA(())],
        compiler_params=_params("arbitrary"),
        name="combine_shared_residual",
    )(dest, hp, wt, w_sg, w_su, w_sd, x, mod_rows, yp)


def _rmsnorm_kernel(x_ref, w_ref, out_ref):
    x = x_ref[...]
    out_ref[...] = x * lax.rsqrt(jnp.mean(x * x, axis=-1, keepdims=True) + EPS) * w_ref[...]


def rmsnorm(x, w):
    nb, S, D = x.shape
    ts = min(512, S)
    return pl.pallas_call(
        _rmsnorm_kernel,
        out_shape=jax.ShapeDtypeStruct((nb, S, D), F32),
        grid=(nb, S // ts),
        in_specs=[pl.BlockSpec((None, ts, D), lambda b, s: (b, s, 0)),
                  pl.BlockSpec((1, D), lambda b, s: (0, 0))],
        out_specs=pl.BlockSpec((None, ts, D), lambda b, s: (b, s, 0)),
        compiler_params=_params("parallel", "parallel"),
        name="final_rmsnorm",
    )(x, w.reshape(1, D))


def moe_block(x, norm_w, mod_rows, layer, w_router, router_bias, w_exp_gate, w_exp_up, w_exp_down,
              w_sh_gate, w_sh_up, w_sh_down):
    hp, logits = norm_router(x, norm_w, mod_rows, layer, w_router)
    eid, pos, wt, counts = route(logits, router_bias)
    counts = counts[:, 0]
    offsets = jnp.cumsum(counts) - counts
    dest = destination_rows(offsets, eid, pos)
    n_tok = hp.shape[0]
    token_of_row = (jnp.argsort(dest.reshape(-1)) % n_tok).astype(I32)[None, :]
    n_rows = token_of_row.shape[1]
    segments = expert_segments(counts, n_rows, min(EXPERT_TILE, n_rows))
    yp = routed_experts(hp, token_of_row, segments, w_exp_gate, w_exp_up, w_exp_down, layer)
    return combine_shared_residual(yp, dest, wt.T, hp, w_sh_gate, w_sh_up, w_sh_down, x, mod_rows, layer, 5)


def kernel(x, c, norm_mix, norm_ffn, w_ada, b_ada, w_in, hgrn_lower_bounds, hgrn_out_norm, w_proj_a, w_proj_b, w_out, w_router, router_bias, w_exp_gate, w_exp_up, w_exp_down, w_sh_gate, w_sh_up, w_sh_down, norm_final):
    nb, S, D = x.shape
    depth = w_in.shape[0]
    a_width = w_proj_a.shape[1]
    b_width = w_proj_b.shape[1]
    in_cols = w_in.shape[2]
    qkv_cols = 3 * a_width
    rest_cols = in_cols - qkv_cols
    col_qb = 0
    col_fb = col_qb + b_width
    col_ib = col_fb + b_width
    col_gb = col_ib + b_width
    col_gate_a = col_gb + b_width
    col_gate_b = col_gate_a + D
    dilations = tuple(d for _, d in DILATED_PATTERNS)
    assert all(w // d == SPAN for w, d in DILATED_PATTERNS)

    lb_sm = jax.nn.softmax(hgrn_lower_bounds.astype(F32), axis=0)
    lb_all = jnp.cumsum(lb_sm, axis=0) - lb_sm[0:1]
    cos, sin = rope_tables(S)

    mod = adaln_modulation(c, w_ada, b_ada)
    mod_rows = mod.reshape(depth * nb * N_MOD, 1, D)

    for l in range(depth):
        h = norm_modulate(x, norm_mix[:, None, :], mod_rows, l, 0, 1)
        qkvs = qkv_projection(h, w_in, l, qkv_cols, 2 * a_width, cos, sin, dilations)
        rest = matmul_cols(h.reshape(nb * S, D), w_in, l, qkv_cols, rest_cols, BF16)

        outs, lses = [], []
        for qkv in qkvs:
            o_g, lse_g = dilated_attention_branch(qkv, a_width)
            outs.append(o_g)
            lses.append(jnp.transpose(lse_g, (0, 2, 1, 3)).reshape(nb, S, -1))
        o_a = attention_merge(outs, lses, dilations)

        o_b = hgrn2(rest.reshape(nb, S, rest_cols), col_qb, col_fb, col_ib, col_gb, b_width,
                    lb_all[l][None, :], hgrn_out_norm[l][None, :])

        u = merge_projection(o_a.reshape(nb * S, a_width), o_b.reshape(nb * S, b_width),
                             w_proj_a, w_proj_b, l, rest, col_gate_a, col_gate_b)
        x = projection_residual(u.reshape(nb, S, D), w_out, x, mod_rows, l, 2)

        x = moe_block(x, norm_ffn[:, None, :], mod_rows, l, w_router[l], router_bias[l],
                      w_exp_gate, w_exp_up, w_exp_down, w_sh_gate, w_sh_up, w_sh_down)
    return rmsnorm(x, norm_final)
```

```python
import functools

import jax
import jax.numpy as jnp
import numpy as np
from jax import lax
from jax.experimental import pallas as pl
from jax.experimental.pallas import tpu as pltpu

HEAD_DIM = 128
DILATED_PATTERNS = ((128, 1), (512, 4), (2048, 16))
SPAN = 128
ROPE_THETA = 10000.0
N_EXPERTS = 64
TOP_K = 8
N_GROUPS = 8
TOPK_GROUPS = 4
GROUP_SIZE = N_EXPERTS // N_GROUPS
ROUTED_SCALE = 2.5
N_MOD = 6
EPS = 1e-6

LANES = 128
SUBLANES = 8
VMEM_LIMIT_BYTES = 56 * 1024 * 1024

HGRN_CHUNK = 128
HGRN_LEVELS = 7

EXPERT_TILE = 512

BF16 = jnp.bfloat16
F32 = jnp.float32
U32 = jnp.uint32
I32 = jnp.int32

_NT = (((1,), (1,)), ((), ()))
_TN = (((0,), (0,)), ((), ()))


def _params(*sem):
    return pltpu.CompilerParams(dimension_semantics=sem, vmem_limit_bytes=VMEM_LIMIT_BYTES)


def _sigmoid(x):
    return 1.0 / (1.0 + jnp.exp(-x))


def _silu(x):
    return x * _sigmoid(x)


def _split_bf16(x):
    hi = x.astype(BF16)
    lo = (x - hi.astype(F32)).astype(BF16)
    return hi, lo


def _bf16_bits(x):
    return (lax.bitcast_convert_type(x, U32) + jnp.uint32(0x8000)) & jnp.uint32(0xFFFF0000)


def _pack_pair(lo, hi):
    return ((lax.bitcast_convert_type(lo, U32) + jnp.uint32(0x8000)) >> 16) | _bf16_bits(hi)


def _unpack_pair(p):
    lo = lax.bitcast_convert_type(p << 16, F32)
    hi = lax.bitcast_convert_type(p & jnp.uint32(0xFFFF0000), F32)
    return lo, hi


def _adaln_kernel(cb_ref, w_ref, bias_ref, out_ref, cs_ref, *, kc):
    K, bn = w_ref.shape
    nb = cb_ref.shape[0]

    @pl.when((pl.program_id(0) == 0) & (pl.program_id(1) == 0))
    def _():
        cs_ref[...] = _silu(cb_ref[...])

    for j in range(bn // LANES):
        cols = slice(j * LANES, (j + 1) * LANES)

        def body(i, accs):
            k0 = pl.multiple_of(i * kc, kc)
            w = w_ref[pl.ds(k0, kc), cols]
            out = []
            for b in range(nb):
                p = (w * cs_ref[b, pl.ds(k0, kc), :]).reshape(kc // SUBLANES, SUBLANES, LANES)
                out.append(accs[b] + jnp.sum(p, axis=0))
            return tuple(out)

        accs = lax.fori_loop(0, K // kc, body,
                             tuple(jnp.zeros((SUBLANES, LANES), F32) for _ in range(nb)))
        for b in range(nb):
            out_ref[b:b + 1, cols] = jnp.sum(accs[b], axis=0, keepdims=True) + bias_ref[:, cols]


def adaln_modulation(c, w_ada, b_ada):
    nl, K, N = w_ada.shape
    nb = c.shape[0]
    bn = min(512, N)
    kc = min(256, K)
    cb = jnp.broadcast_to(c[:, :, None], (nb, K, LANES))
    return pl.pallas_call(
        functools.partial(_adaln_kernel, kc=kc),
        out_shape=jax.ShapeDtypeStruct((nl, nb, N), F32),
        grid=(nl, N // bn),
        in_specs=[
            pl.BlockSpec((nb, K, LANES), lambda l, n: (0, 0, 0)),
            pl.BlockSpec((None, K, bn), lambda l, n: (l, 0, n)),
            pl.BlockSpec((None, 1, bn), lambda l, n: (l, 0, n)),
        ],
        out_specs=pl.BlockSpec((None, nb, bn), lambda l, n: (l, 0, n)),
        scratch_shapes=[pltpu.VMEM((nb, K, LANES), F32)],
        compiler_params=_params("arbitrary", "arbitrary"),
        name="adaln_modulation",
    )(cb, w_ada, b_ada.reshape(nl, 1, N))


def _norm_mod(x, w, shift, scale):
    y = x * lax.rsqrt(jnp.mean(x * x, axis=-1, keepdims=True) + EPS) * w
    return y * (1.0 + scale) + shift


def _norm_mod_kernel(x_ref, w_ref, shift_ref, scale_ref, out_ref):
    out_ref[...] = _norm_mod(x_ref[...], w_ref[...], shift_ref[...], scale_ref[...]).astype(out_ref.dtype)


def _mod_spec(D, which, layer, nb):
    return pl.BlockSpec((None, 1, D), lambda b, *_: ((layer * nb + b) * N_MOD + which, 0, 0))


def norm_modulate(x, norm_w, mod_rows, layer, which_shift, which_scale):
    nb, S, D = x.shape
    ts = min(512, S)
    return pl.pallas_call(
        _norm_mod_kernel,
        out_shape=jax.ShapeDtypeStruct((nb, S, D), BF16),
        grid=(nb, S // ts),
        in_specs=[
            pl.BlockSpec((None, ts, D), lambda b, s: (b, s, 0)),
            pl.BlockSpec((None, 1, D), lambda b, s: (layer, 0, 0)),
            _mod_spec(D, which_shift, layer, nb),
            _mod_spec(D, which_scale, layer, nb),
        ],
        out_specs=pl.BlockSpec((None, ts, D), lambda b, s: (b, s, 0)),
        compiler_params=_params("parallel", "parallel"),
        name="norm_modulate",
    )(x, norm_w, mod_rows, mod_rows)


def _qkv_proj_kernel(x_ref, w_ref, cos_ref, sin_ref, *refs, n_rope_tiles, dilations):
    out_refs, scr = refs[:-1], refs[-1]
    n_heads, bm, _ = scr.shape
    y = jnp.dot(x_ref[...], w_ref[...].astype(BF16), preferred_element_type=F32)
    n = pl.program_id(2)

    @pl.when(n < n_rope_tiles)
    def _():
        cos = cos_ref[...]
        sin = sin_ref[...]
        for h in range(n_heads):
            t = y[:, h * HEAD_DIM:(h + 1) * HEAD_DIM]
            scr[h] = t * cos + pltpu.roll(t, HEAD_DIM // 2, 1) * sin

    @pl.when(n >= n_rope_tiles)
    def _():
        for h in range(n_heads):
            scr[h] = y[:, h * HEAD_DIM:(h + 1) * HEAD_DIM]

    for d, o_ref in zip(dilations, out_refs):
        for r in range(d):
            for h in range(n_heads):
                o_ref[r, :, h * HEAD_DIM:(h + 1) * HEAD_DIM] = (
                    scr[h, pl.ds(r, bm // d, stride=d), :].astype(o_ref.dtype))


def rope_tables(S):
    half = HEAD_DIM // 2
    inv = ROPE_THETA ** (-jnp.arange(half, dtype=F32) / half)
    ang = jnp.arange(S, dtype=F32)[:, None] * inv[None, :]
    cos, sin = jnp.cos(ang), jnp.sin(ang)
    return jnp.concatenate([cos, cos], axis=-1), jnp.concatenate([-sin, sin], axis=-1)


def qkv_projection(h, w_in, layer, qkv_cols, rope_cols, cos, sin, dilations):
    nb, S, D = h.shape
    bm, bn = min(1024, S), min(512, qkv_cols)
    outs = tuple(jax.ShapeDtypeStruct((nb, d, S // d, qkv_cols), BF16) for d in dilations)
    out_specs = tuple(pl.BlockSpec((None, d, bm // d, bn), lambda b, m, n: (b, 0, m, n)) for d in dilations)
    return pl.pallas_call(
        functools.partial(_qkv_proj_kernel, n_rope_tiles=rope_cols // bn, dilations=dilations),
        out_shape=outs,
        grid=(nb, S // bm, qkv_cols // bn),
        in_specs=[pl.BlockSpec((None, bm, D), lambda b, m, n: (b, m, 0)),
                  pl.BlockSpec((None, D, bn), lambda b, m, n: (layer, 0, n)),
                  pl.BlockSpec((bm, HEAD_DIM), lambda b, m, n: (m, 0)),
                  pl.BlockSpec((bm, HEAD_DIM), lambda b, m, n: (m, 0))],
        out_specs=out_specs,
        scratch_shapes=[pltpu.VMEM((bn // HEAD_DIM, bm, HEAD_DIM), F32)],
        compiler_params=_params("parallel", "parallel", "arbitrary"),
        name="qkv_projection",
    )(h, w_in, cos, sin)


def _matmul_kernel(x_ref, w_ref, out_ref):
    out_ref[...] = jnp.dot(x_ref[...], w_ref[...].astype(BF16),
                           preferred_element_type=F32).astype(out_ref.dtype)


def matmul_cols(x, w, layer, col0, ncols, out_dtype):
    M, K = x.shape
    bm, bn = min(1024, M), min(512, ncols)
    return pl.pallas_call(
        _matmul_kernel,
        out_shape=jax.ShapeDtypeStruct((M, ncols), out_dtype),
        grid=(M // bm, ncols // bn),
        in_specs=[pl.BlockSpec((bm, K), lambda m, n: (m, 0)),
                  pl.BlockSpec((None, K, bn), lambda m, n: (layer, 0, col0 // bn + n))],
        out_specs=pl.BlockSpec((bm, bn), lambda m, n: (m, n)),
        compiler_params=_params("parallel", "arbitrary"),
        name="matmul_cols",
    )(x, w)


def _attn_kernel(q_ref, kp_ref, kc_ref, vp_ref, vc_ref, o_ref, lse_ref):
    i = pl.program_id(2)
    n_heads = q_ref.shape[-1] // HEAD_DIM
    qi = lax.broadcasted_iota(I32, (SPAN, 2 * SPAN), 0)
    kj = lax.broadcasted_iota(I32, (SPAN, 2 * SPAN), 1)
    mask = ((kj < SPAN) & (kj >= qi) & (i > 0)) | ((kj >= SPAN) & ((kj - SPAN) <= qi))
    scale = HEAD_DIM ** -0.5
    s = []
    for h in range(n_heads):
        cols = slice(h * HEAD_DIM, (h + 1) * HEAD_DIM)
        k_h = jnp.concatenate([kp_ref[:, cols], kc_ref[:, cols]], axis=0)
        s.append(lax.dot_general(q_ref[:, cols], k_h, _NT, preferred_element_type=F32))
    s = jnp.stack(s, axis=0)
    s = jnp.where(mask[None], s * scale, -jnp.inf)
    m = jnp.max(s, axis=-1, keepdims=True)
    p = jnp.exp(s - m).astype(BF16)
    ones = jnp.ones((2 * SPAN, HEAD_DIM), BF16)
    lses = []
    for h in range(n_heads):
        cols = slice(h * HEAD_DIM, (h + 1) * HEAD_DIM)
        v_h = jnp.concatenate([vp_ref[:, cols], vc_ref[:, cols]], axis=0)
        oe = jnp.dot(p[h], jnp.concatenate([v_h, ones], axis=1), preferred_element_type=F32)
        den = oe[:, HEAD_DIM:]
        o_ref[:, cols] = (oe[:, :HEAD_DIM] / den).astype(o_ref.dtype)
        lses.append(m[h] + jnp.log(den[:, :1]))
    lse_ref[...] = jnp.concatenate(lses, axis=-1)


def dilated_attention_branch(qkv, width):
    nb, d, L, _ = qkv.shape
    nblk = L // SPAN
    n_heads = width // HEAD_DIM

    def spec(col, prev):
        if prev:
            return pl.BlockSpec((None, None, SPAN, width), lambda b, r, i: (b, r, jnp.maximum(i - 1, 0), col))
        return pl.BlockSpec((None, None, SPAN, width), lambda b, r, i: (b, r, i, col))

    return pl.pallas_call(
        _attn_kernel,
        out_shape=(jax.ShapeDtypeStruct((nb, d, L, width), BF16),
                   jax.ShapeDtypeStruct((nb, d, L, n_heads), F32)),
        grid=(nb, d, nblk),
        in_specs=[spec(0, False), spec(1, True), spec(1, False), spec(2, True), spec(2, False)],
        out_specs=(pl.BlockSpec((None, None, SPAN, width), lambda b, r, i: (b, r, i, 0)),
                   pl.BlockSpec((None, None, SPAN, n_heads), lambda b, r, i: (b, r, i, 0))),
        compiler_params=_params("parallel", "parallel", "arbitrary"),
        name=f"dilated_attention_d{d}",
    )(qkv, qkv, qkv, qkv, qkv)


def _attn_merge_kernel(*refs, dilations):
    n = len(dilations)
    o_refs, lse_refs, out_ref, scr_refs = refs[:n], refs[n:2 * n], refs[2 * n], refs[2 * n + 1:]
    ts = out_ref.shape[0]
    nat = []
    for g, d in enumerate(dilations):
        if d == 1:
            nat.append(None)
            continue
        scr = scr_refs[len([x for x in nat if x is not None])]
        for r in range(d):
            for h in range(scr.shape[0]):
                scr[h, pl.ds(r, ts // d, stride=d), :] = (
                    o_refs[g][r, :, h * HEAD_DIM:(h + 1) * HEAD_DIM].astype(F32))
        nat.append(scr)
    lses = [r[...] for r in lse_refs]
    m = functools.reduce(jnp.maximum, lses)
    es = [jnp.exp(l - m) for l in lses]
    tot = functools.reduce(jnp.add, es)
    ws = [e / tot for e in es]
    for h in range(out_ref.shape[-1] // HEAD_DIM):
        cols = slice(h * HEAD_DIM, (h + 1) * HEAD_DIM)
        acc = jnp.zeros((ts, HEAD_DIM), F32)
        for g in range(n):
            o = o_refs[g][0, :, cols].astype(F32) if nat[g] is None else nat[g][h]
            acc = acc + ws[g][:, h:h + 1] * o
        out_ref[:, cols] = acc.astype(out_ref.dtype)


def attention_merge(outs, lses, dilations):
    nb, _, S, width = outs[dilations.index(1)].shape
    n_heads = lses[0].shape[-1]
    ts = min(512, S)
    o_specs = [pl.BlockSpec((None, d, ts // d, width), lambda b, s: (b, 0, s, 0)) for d in dilations]
    l_spec = pl.BlockSpec((None, ts, n_heads), lambda b, s: (b, s, 0))
    return pl.pallas_call(
        functools.partial(_attn_merge_kernel, dilations=dilations),
        out_shape=jax.ShapeDtypeStruct((nb, S, width), BF16),
        grid=(nb, S // ts),
        in_specs=o_specs + [l_spec] * len(lses),
        out_specs=pl.BlockSpec((None, ts, width), lambda b, s: (b, s, 0)),
        scratch_shapes=[pltpu.VMEM((n_heads, ts, HEAD_DIM), F32) for d in dilations if d > 1],
        compiler_params=_params("parallel", "parallel"),
        name="attention_merge",
    )(*outs, *lses)


def _hgrn_sum_matrix(C, levels):
    t = np.arange(C)[:, None]
    u = np.arange(C)[None, :]
    blocks = [(u <= t), (u > t)]
    for j in range(levels):
        half = C >> (j + 1)
        mid = (t // (2 * half)) * (2 * half) + half - 1
        upper = (t // half) % 2 == 1
        blocks.append(np.where(upper, (u > mid) & (u <= t), (u > t) & (u <= mid)))
    return np.concatenate(blocks, axis=0).astype(np.float32)


def _hgrn_kernel(q_ref, f_ref, i_ref, g_ref, lb_ref, nw_ref, sm_ref, out_ref, state_ref, *, n_chunks):
    C = HGRN_CHUNK

    @pl.when(pl.program_id(2) == 0)
    def _():
        state_ref[...] = jnp.zeros_like(state_ref)

    ti = lax.broadcasted_iota(I32, (C, C), 0)
    si = lax.broadcasted_iota(I32, (C, C), 1)
    xor = ti ^ si
    lower = si < ti
    lb = lb_ref[...]
    nw = nw_ref[...]
    state = state_ref[...]

    for c in range(n_chunks):
        rows = slice(c * C, (c + 1) * C)
        q = _silu(q_ref[rows, :].astype(F32))
        f = lb + (1.0 - lb) * _sigmoid(f_ref[rows, :].astype(F32))
        k = 1.0 - f
        g = jnp.log(f)
        v = i_ref[rows, :]
        g_hi, g_lo = _split_bf16(g)
        ghl = jnp.concatenate([g_hi, g_lo], axis=-1)
        e2 = jnp.dot(sm_ref[...], ghl, preferred_element_type=F32)
        e = e2[:, :HEAD_DIM] + e2[:, HEAD_DIM:]

        scores = jnp.where(ti == si, lax.dot_general(q.astype(BF16), k.astype(BF16), _NT,
                                                     preferred_element_type=F32), 0.0)
        for j in range(HGRN_LEVELS):
            a = jnp.exp(e[(2 + j) * C:(3 + j) * C, :])
            s_j = lax.dot_general((q * a).astype(BF16), (k * a).astype(BF16), _NT,
                                  preferred_element_type=F32)
            scores = scores + jnp.where(lower & ((xor >> (HGRN_LEVELS - 1 - j)) == 1), s_j, 0.0)

        o = jnp.dot(scores.astype(BF16), v, preferred_element_type=F32)
        decay = jnp.exp(e[0:C, :])
        o = o + lax.dot_general((q * decay).astype(BF16), state.astype(BF16), _NT,
                                preferred_element_type=F32)
        k_end = (k * jnp.exp(e[C:2 * C, :])).astype(BF16)
        state = state * decay[C - 1:C, :] + lax.dot_general(v, k_end, _TN, preferred_element_type=F32)

        y = o * lax.rsqrt(jnp.mean(o * o, axis=-1, keepdims=True) + EPS) * nw
        out_ref[rows, :] = (y * _silu(g_ref[rows, :].astype(F32))).astype(out_ref.dtype)
    state_ref[...] = state


def hgrn2(proj, col_q, col_f, col_i, col_g, width, lb, out_norm):
    nb, S, _ = proj.shape
    n_heads = width // HEAD_DIM
    C = HGRN_CHUNK
    ts = min(512, S)
    sm = jnp.asarray(_hgrn_sum_matrix(C, HGRN_LEVELS), dtype=BF16)

    def col_spec(col0):
        blk = col0 // HEAD_DIM
        return pl.BlockSpec((None, ts, HEAD_DIM), lambda b, h, s: (b, s, blk + h))

    return pl.pallas_call(
        functools.partial(_hgrn_kernel, n_chunks=ts // C),
        out_shape=jax.ShapeDtypeStruct((nb, S, width), BF16),
        grid=(nb, n_heads, S // ts),
        in_specs=[col_spec(col_q), col_spec(col_f), col_spec(col_i), col_spec(col_g),
                  pl.BlockSpec((1, HEAD_DIM), lambda b, h, s: (0, h)),
                  pl.BlockSpec((1, HEAD_DIM), lambda b, h, s: (0, 0)),
                  pl.BlockSpec(sm.shape, lambda b, h, s: (0, 0))],
        out_specs=pl.BlockSpec((None, ts, HEAD_DIM), lambda b, h, s: (b, s, h)),
        scratch_shapes=[pltpu.VMEM((HEAD_DIM, HEAD_DIM), F32)],
        compiler_params=_params("parallel", "parallel", "arbitrary"),
        name="hgrn2",
    )(proj, proj, proj, proj, lb, out_norm, sm)


def _merge_proj_kernel(oa_ref, ob_ref, wa_ref, wb_ref, ga_ref, gb_ref, out_ref):
    ya = jnp.dot(oa_ref[...], wa_ref[...].astype(BF16), preferred_element_type=F32)
    yb = jnp.dot(ob_ref[...], wb_ref[...].astype(BF16), preferred_element_type=F32)
    u = _sigmoid(ga_ref[...].astype(F32)) * ya + _sigmoid(gb_ref[...].astype(F32)) * yb
    out_ref[...] = u.astype(out_ref.dtype)


def merge_projection(o_a, o_b, w_a, w_b, layer, proj, col_ga, col_gb):
    M, K = o_a.shape
    N = w_a.shape[2]
    bm, bn = min(1024, M), min(512, N)
    return pl.pallas_call(
        _merge_proj_kernel,
        out_shape=jax.ShapeDtypeStruct((M, N), BF16),
        grid=(M // bm, N // bn),
        in_specs=[pl.BlockSpec((bm, K), lambda m, n: (m, 0)),
                  pl.BlockSpec((bm, K), lambda m, n: (m, 0)),
                  pl.BlockSpec((None, K, bn), lambda m, n: (layer, 0, n)),
                  pl.BlockSpec((None, K, bn), lambda m, n: (layer, 0, n)),
                  pl.BlockSpec((bm, bn), lambda m, n: (m, col_ga // bn + n)),
                  pl.BlockSpec((bm, bn), lambda m, n: (m, col_gb // bn + n))],
        out_specs=pl.BlockSpec((bm, bn), lambda m, n: (m, n)),
        compiler_params=_params("parallel", "arbitrary"),
        name="merge_projection",
    )(o_a, o_b, w_a, w_b, proj, proj)


def _proj_residual_kernel(u_ref, w_ref, x_ref, gate_ref, out_ref):
    y = jnp.dot(u_ref[...], w_ref[...].astype(BF16), preferred_element_type=F32)
    out_ref[...] = x_ref[...] + gate_ref[...] * y


def projection_residual(u, w, x, mod_rows, layer, which_gate):
    nb, S, K = u.shape
    D = w.shape[2]
    bm, bn = min(1024, S), min(512, D)
    return pl.pallas_call(
        _proj_residual_kernel,
        out_shape=jax.ShapeDtypeStruct((nb, S, D), F32),
        grid=(nb, S // bm, D // bn),
        in_specs=[pl.BlockSpec((None, bm, K), lambda b, m, n: (b, m, 0)),
                  pl.BlockSpec((None, K, bn), lambda b, m, n: (layer, 0, n)),
                  pl.BlockSpec((None, bm, bn), lambda b, m, n: (b, m, n)),
                  pl.BlockSpec((None, 1, bn), lambda b, m, n: ((layer * nb + b) * N_MOD + which_gate, 0, n))],
        out_specs=pl.BlockSpec((None, bm, bn), lambda b, m, n: (b, m, n)),
        compiler_params=_params("parallel", "parallel", "arbitrary"),
        name="projection_residual",
    )(u, w, x, mod_rows)


def _norm_router_kernel(x_ref, w_ref, shift_ref, scale_ref, wr_hi_ref, wr_lo_ref, hp_ref, logit_ref):
    h = _norm_mod(x_ref[...], w_ref[...], shift_ref[...], scale_ref[...])
    half = h.shape[1] // 2
    bits = _bf16_bits(h)
    hp_ref[...] = (bits[:, :half] >> 16) | bits[:, half:]
    h_r = lax.bitcast_convert_type(bits, F32)
    h_hi = h_r.astype(BF16)
    h_lo = (h - h_r).astype(BF16)
    logit_ref[...] = (lax.dot_general(wr_hi_ref[...], h_hi, _NT, preferred_element_type=F32)
                      + lax.dot_general(wr_hi_ref[...], h_lo, _NT, preferred_element_type=F32)
                      + lax.dot_general(wr_lo_ref[...], h_hi, _NT, preferred_element_type=F32))


def norm_router(x, norm_w, mod_rows, layer, w_router):
    nb, S, D = x.shape
    E = w_router.shape[1]
    ts = min(512, S)
    nts = S // ts
    wr_hi, wr_lo = _split_bf16(w_router.T)
    return pl.pallas_call(
        _norm_router_kernel,
        out_shape=(jax.ShapeDtypeStruct((nb * S, D // 2), U32), jax.ShapeDtypeStruct((E, nb * S), F32)),
        grid=(nb, nts),
        in_specs=[pl.BlockSpec((None, ts, D), lambda b, s: (b, s, 0)),
                  pl.BlockSpec((None, 1, D), lambda b, s: (layer, 0, 0)),
                  _mod_spec(D, 3, layer, nb),
                  _mod_spec(D, 4, layer, nb),
                  pl.BlockSpec((E, D), lambda b, s: (0, 0)),
                  pl.BlockSpec((E, D), lambda b, s: (0, 0))],
        out_specs=(pl.BlockSpec((ts, D // 2), lambda b, s: (b * nts + s, 0)),
                   pl.BlockSpec((E, ts), lambda b, s: (0, b * nts + s))),
        compiler_params=_params("parallel", "parallel"),
        name="norm_router",
    )(x, norm_w, mod_rows, mod_rows, wr_hi, wr_lo)


def _rank_lt(vals, n_rows, limit):
    ridx = lax.broadcasted_iota(I32, vals.shape, 0)
    cnt = jnp.zeros(vals.shape, I32)
    for r in range(n_rows):
        other = vals[r:r + 1, :]
        beats = (other > vals) | ((other == vals) & (r < ridx))
        cnt = cnt + beats.astype(I32)
    return cnt < limit


def _route_kernel(logit_ref, bias_ref, trie_ref, trit_ref, eid_ref, pos_ref, wt_ref, cnt_ref, carry_ref):
    @pl.when(pl.program_id(0) == 0)
    def _():
        carry_ref[...] = jnp.zeros_like(carry_ref)

    scores = _sigmoid(logit_ref[...])
    sel = scores + bias_ref[...]
    E, T = sel.shape
    sub = lax.broadcasted_iota(I32, (GROUP_SIZE, T), 0)
    gscores = []
    for g in range(N_GROUPS):
        v = sel[g * GROUP_SIZE:(g + 1) * GROUP_SIZE, :]
        m1 = jnp.max(v, axis=0, keepdims=True)
        first = jnp.min(jnp.where(v == m1, sub, GROUP_SIZE), axis=0, keepdims=True)
        m2 = jnp.max(jnp.where(sub == first, -jnp.inf, v), axis=0, keepdims=True)
        gscores.append(m1 + m2)
    gsc = jnp.concatenate(gscores, axis=0)
    gkeep = _rank_lt(gsc, N_GROUPS, TOPK_GROUPS)
    ekeep = jnp.concatenate(
        [jnp.broadcast_to(gkeep[g:g + 1, :], (GROUP_SIZE, T)) for g in range(N_GROUPS)], axis=0)
    masked = jnp.where(ekeep, sel, -jnp.inf)
    chosen = _rank_lt(masked, N_EXPERTS, TOP_K)
    w = jnp.where(chosen, scores, 0.0)
    combine = w / jnp.sum(w, axis=0, keepdims=True) * ROUTED_SCALE

    cf = jnp.where(chosen, 1.0, 0.0)
    cb = cf.astype(BF16)
    rank = jnp.dot(trie_ref[...], cb, preferred_element_type=F32)
    local = jnp.dot(cb, trit_ref[...], preferred_element_type=F32)
    carry = carry_ref[:, :1]
    pos = carry + local
    eidx = lax.broadcasted_iota(I32, (E, T), 0).astype(F32)
    eids, poss, wts = [], [], []
    for k in range(TOP_K):
        sel_k = chosen & (rank == k)
        eids.append(jnp.sum(jnp.where(sel_k, eidx, 0.0), axis=0, keepdims=True))
        poss.append(jnp.sum(jnp.where(sel_k, pos, 0.0), axis=0, keepdims=True))
        wts.append(jnp.sum(jnp.where(sel_k, combine, 0.0), axis=0, keepdims=True))
    eid_ref[...] = jnp.concatenate(eids, axis=0).astype(I32)
    pos_ref[...] = jnp.concatenate(poss, axis=0).astype(I32)
    wt_ref[...] = jnp.concatenate(wts, axis=0)
    total = carry + jnp.sum(cf, axis=1, keepdims=True)
    carry_ref[...] = jnp.broadcast_to(total, carry_ref.shape)
    cnt_ref[...] = jnp.broadcast_to(total, cnt_ref.shape).astype(I32)


def route(logits, router_bias):
    E, T = logits.shape
    tt = min(512, T)
    trie = jnp.asarray(np.tril(np.ones((E, E), np.float32), -1), BF16)
    trit = jnp.asarray(np.triu(np.ones((tt, tt), np.float32), 1), BF16)
    kt = pl.BlockSpec((TOP_K, tt), lambda t: (0, t))
    return pl.pallas_call(
        _route_kernel,
        out_shape=(jax.ShapeDtypeStruct((TOP_K, T), I32), jax.ShapeDtypeStruct((TOP_K, T), I32),
                   jax.ShapeDtypeStruct((TOP_K, T), F32), jax.ShapeDtypeStruct((E, LANES), I32)),
        grid=(T // tt,),
        in_specs=[pl.BlockSpec((E, tt), lambda t: (0, t)),
                  pl.BlockSpec((E, 1), lambda t: (0, 0)),
                  pl.BlockSpec((E, E), lambda t: (0, 0)),
                  pl.BlockSpec((tt, tt), lambda t: (0, 0))],
        out_specs=(kt, kt, kt, pl.BlockSpec((E, LANES), lambda t: (0, 0))),
        scratch_shapes=[pltpu.VMEM((E, LANES), F32)],
        compiler_params=_params("arbitrary"),
        name="route",
    )(logits, router_bias.reshape(E, 1), trie, trit)


def _dest_kernel(off_ref, eid_ref, pos_ref, dest_ref):
    eid = eid_ref[...]
    acc = pos_ref[...]
    for e in range(N_EXPERTS):
        acc = acc + jnp.where(eid == e, off_ref[e], 0)
    dest_ref[...] = acc


def destination_rows(offsets, eid, pos):
    K, T = eid.shape
    tt = min(2048, T)
    kt = pl.BlockSpec((K, tt), lambda t, off: (0, t))
    return pl.pallas_call(
        _dest_kernel,
        out_shape=jax.ShapeDtypeStruct((K, T), I32),
        grid_spec=pltpu.PrefetchScalarGridSpec(num_scalar_prefetch=1, grid=(T // tt,),
                                               in_specs=[kt, kt], out_specs=kt),
        compiler_params=_params("parallel"),
        name="destination_rows",
    )(offsets, eid, pos)


def _new_expert(eid_ref):
    j = pl.program_id(0)
    return (j == 0) | (eid_ref[j] != eid_ref[jnp.maximum(j - 1, 0)])


def _experts_up_kernel(eid_ref, used_ref, tok_ref, nxt_ref, hp_hbm, wg_ref, wu_ref, act_ref,
                       xbuf, wg_s, wu_s, sem):
    @pl.when(_new_expert(eid_ref))
    def _():
        wg_s[...] = wg_ref[...].astype(BF16)
        wu_s[...] = wu_ref[...].astype(BF16)

    j = pl.program_id(0)
    n_used = used_ref[0]
    tm, half = xbuf.shape[1:]
    slot = j % 2

    def row_copy(idx_ref, i, s):
        return pltpu.make_async_copy(hp_hbm.at[pl.ds(idx_ref[0, i], 1)], xbuf.at[s, pl.ds(i, 1)], sem.at[s])

    def tile_step(request_next):
        @pl.when(j == 0)
        def _():
            def start(i, carry):
                row_copy(tok_ref, i, 0).start()
                return carry
            lax.fori_loop(0, tm, start, 0, unroll=8)

        def wait(i, carry):
            row_copy(tok_ref, i, slot).wait()
            return carry
        lax.fori_loop(0, tm, wait, 0, unroll=8)

        x_lo, x_hi = _unpack_pair(xbuf[slot])
        x_lo, x_hi = x_lo.astype(BF16), x_hi.astype(BF16)
        gate = (jnp.dot(x_lo, wg_s[:half, :], preferred_element_type=F32)
                + jnp.dot(x_hi, wg_s[half:, :], preferred_element_type=F32))
        up = (jnp.dot(x_lo, wu_s[:half, :], preferred_element_type=F32)
              + jnp.dot(x_hi, wu_s[half:, :], preferred_element_type=F32))
        if request_next:
            for i in range(tm):
                row_copy(nxt_ref, i, 1 - slot).start()
        act_ref[...] = (_silu(gate) * up).astype(act_ref.dtype)

    @pl.when(j + 1 < n_used)
    def _():
        tile_step(True)

    @pl.when(j + 1 == n_used)
    def _():
        tile_step(False)

    @pl.when(j >= n_used)
    def _():
        act_ref[...] = jnp.zeros_like(act_ref)


def _experts_down_kernel(eid_ref, used_ref, act_ref, wd_ref, y_ref, wd_s):
    @pl.when(_new_expert(eid_ref))
    def _():
        wd_s[...] = wd_ref[...].astype(BF16)

    @pl.when(pl.program_id(0) < used_ref[0])
    def _():
        half = y_ref.shape[1]
        act = act_ref[...]
        y_ref[...] = _pack_pair(jnp.dot(act, wd_s[:, :half], preferred_element_type=F32),
                                jnp.dot(act, wd_s[:, half:], preferred_element_type=F32))

    @pl.when(pl.program_id(0) >= used_ref[0])
    def _():
        y_ref[...] = jnp.zeros_like(y_ref)


def expert_tiles(counts, n_assign, tm):
    E = counts.shape[0]
    n_tiles = n_assign // tm + E
    tiles_e = (counts + tm - 1) // tm
    tile_end = jnp.cumsum(tiles_e)
    off_pad = (tile_end - tiles_e) * tm
    off = jnp.cumsum(counts) - counts
    tile_idx = jnp.arange(n_tiles, dtype=I32)
    tile_eid = jnp.minimum(jnp.sum(tile_end[None, :] <= tile_idx[:, None], axis=1), E - 1).astype(I32)
    n_used = tile_end[-1:].astype(I32)
    row = jnp.arange(n_tiles * tm, dtype=I32)
    row_eid = jnp.repeat(tile_eid, tm)
    p = row - off_pad[row_eid]
    valid = (p < counts[row_eid]) & (row < n_used[0] * tm)
    src = jnp.where(valid, off[row_eid] + p, -1).astype(I32)
    return off_pad.astype(I32), tile_eid, n_used, src


def routed_experts(hp, token_of_row, tile_eid, n_used, w_gate, w_up, w_down, layer):
    _, half = hp.shape
    R = token_of_row.shape[1]
    _, E, D, F = w_gate.shape
    n_tiles = tile_eid.shape[0]
    tm = R // n_tiles

    def last_used(j, used):
        return jnp.minimum(j, used[0] - 1)

    def tile_spec(width):
        return pl.BlockSpec((tm, width), lambda j, eid, used: (j, 0))

    def weight_spec(rows, cols):
        return pl.BlockSpec((None, None, rows, cols), lambda j, eid, used: (layer, eid[j], 0, 0))

    act = pl.pallas_call(
        _experts_up_kernel,
        out_shape=jax.ShapeDtypeStruct((R, F), BF16),
        grid_spec=pltpu.PrefetchScalarGridSpec(
            num_scalar_prefetch=2, grid=(n_tiles,),
            in_specs=[pl.BlockSpec((1, tm), lambda j, eid, used: (0, last_used(j, used)),
                                   memory_space=pltpu.SMEM),
                      pl.BlockSpec((1, tm), lambda j, eid, used: (0, last_used(j + 1, used)),
                                   memory_space=pltpu.SMEM),
                      pl.BlockSpec(memory_space=pl.ANY),
                      weight_spec(D, F), weight_spec(D, F)],
            out_specs=tile_spec(F),
            scratch_shapes=[pltpu.VMEM((2, tm, half), U32),
                            pltpu.VMEM((D, F), BF16), pltpu.VMEM((D, F), BF16),
                            pltpu.SemaphoreType.DMA((2,))]),
        compiler_params=_params("arbitrary"),
        name="routed_experts_up",
    )(tile_eid, n_used, token_of_row, token_of_row, hp, w_gate, w_up)
    return pl.pallas_call(
        _experts_down_kernel,
        out_shape=jax.ShapeDtypeStruct((R, half), U32),
        grid_spec=pltpu.PrefetchScalarGridSpec(
            num_scalar_prefetch=2, grid=(n_tiles,),
            in_specs=[tile_spec(F), weight_spec(F, D)],
            out_specs=tile_spec(half),
            scratch_shapes=[pltpu.VMEM((F, D), BF16)]),
        compiler_params=_params("arbitrary"),
        name="routed_experts_down",
    )(tile_eid, n_used, act, w_down)


def _combine_kernel(dest_ref, hp_ref, wt_ref, sg_ref, su_ref, sd_ref, x_ref, gate_ref, yp_hbm, out_ref,
                    buf, sg_s, su_s, sd_s, sem, *, tt):
    @pl.when(pl.program_id(0) == 0)
    def _():
        sg_s[...] = sg_ref[...].astype(BF16)
        su_s[...] = su_ref[...].astype(BF16)
        sd_s[...] = sd_ref[...].astype(BF16)

    def row_copy(i, k):
        return pltpu.make_async_copy(yp_hbm.at[pl.ds(dest_ref[k, i], 1)], buf.at[k, pl.ds(i, 1)], sem)

    def start(i, carry):
        for k in range(TOP_K):
            row_copy(i, k).start()
        return carry

    def wait(i, carry):
        for k in range(TOP_K):
            row_copy(i, k).wait()
        return carry

    lax.fori_loop(0, tt, start, 0)

    half = hp_ref.shape[1]
    h_lo, h_hi = _unpack_pair(hp_ref[...])
    h_lo, h_hi = h_lo.astype(BF16), h_hi.astype(BF16)
    g = (jnp.dot(h_lo, sg_s[:half, :], preferred_element_type=F32)
         + jnp.dot(h_hi, sg_s[half:, :], preferred_element_type=F32))
    u = (jnp.dot(h_lo, su_s[:half, :], preferred_element_type=F32)
         + jnp.dot(h_hi, su_s[half:, :], preferred_element_type=F32))
    act = (_silu(g) * u).astype(BF16)
    y_lo = jnp.dot(act, sd_s[:, :half], preferred_element_type=F32)
    y_hi = jnp.dot(act, sd_s[:, half:], preferred_element_type=F32)

    lax.fori_loop(0, tt, wait, 0)

    for k in range(TOP_K):
        e_lo, e_hi = _unpack_pair(buf[k])
        wk = wt_ref[:, k:k + 1]
        y_lo = y_lo + wk * e_lo
        y_hi = y_hi + wk * e_hi
    gate = gate_ref[...]
    out_ref[:, :half] = x_ref[:, :half] + gate[:, :half] * y_lo
    out_ref[:, half:] = x_ref[:, half:] + gate[:, half:] * y_hi


def combine_shared_residual(yp, dest, wt, hp, w_sg, w_su, w_sd, x, mod_rows, layer, which_gate):
    nb, S, D = x.shape
    T, half = hp.shape
    K = dest.shape[0]
    F = w_sg.shape[2]
    tt = min(128, S)
    nts = S // tt
    xrow = pl.BlockSpec((None, tt, D), lambda t: (t // nts, t % nts, 0))
    return pl.pallas_call(
        functools.partial(_combine_kernel, tt=tt),
        out_shape=jax.ShapeDtypeStruct((nb, S, D), F32),
        grid=(T // tt,),
        in_specs=[pl.BlockSpec((K, tt), lambda t: (0, t), memory_space=pltpu.SMEM),
                  pl.BlockSpec((tt, half), lambda t: (t, 0)),
                  pl.BlockSpec((tt, K), lambda t: (t, 0)),
                  pl.BlockSpec((None, D, F), lambda t: (layer, 0, 0)),
                  pl.BlockSpec((None, D, F), lambda t: (layer, 0, 0)),
                  pl.BlockSpec((None, F, D), lambda t: (layer, 0, 0)),
                  xrow,
                  pl.BlockSpec((None, 1, D), lambda t: ((layer * nb + t // nts) * N_MOD + which_gate, 0, 0)),
                  pl.BlockSpec(memory_space=pl.ANY)],
        out_specs=xrow,
        scratch_shapes=[pltpu.VMEM((K, tt, half), U32),
                        pltpu.VMEM((D, F), BF16), pltpu.VMEM((D, F), BF16), pltpu.VMEM((F, D), BF16),
                        pltpu.SemaphoreType.DMA(())],
        compiler_params=_params("arbitrary"),
        name="combine_shared_residual",
    )(dest, hp, wt, w_sg, w_su, w_sd, x, mod_rows, yp)


def _rmsnorm_kernel(x_ref, w_ref, out_ref):
    x = x_ref[...]
    out_ref[...] = x * lax.rsqrt(jnp.mean(x * x, axis=-1, keepdims=True) + EPS) * w_ref[...]


def rmsnorm(x, w):
    nb, S, D = x.shape
    ts = min(512, S)
    return pl.pallas_call(
        _rmsnorm_kernel,
        out_shape=jax.ShapeDtypeStruct((nb, S, D), F32),
        grid=(nb, S // ts),
        in_specs=[pl.BlockSpec((None, ts, D), lambda b, s: (b, s, 0)),
                  pl.BlockSpec((1, D), lambda b, s: (0, 0))],
        out_specs=pl.BlockSpec((None, ts, D), lambda b, s: (b, s, 0)),
        compiler_params=_params("parallel", "parallel"),
        name="final_rmsnorm",
    )(x, w.reshape(1, D))


def moe_block(x, norm_w, mod_rows, layer, w_router, router_bias, w_exp_gate, w_exp_up, w_exp_down,
              w_sh_gate, w_sh_up, w_sh_down):
    hp, logits = norm_router(x, norm_w, mod_rows, layer, w_router)
    eid, pos, wt, counts = route(logits, router_bias)
    counts = counts[:, 0]
    n_tok = hp.shape[0]
    off_pad, tile_eid, n_used, src = expert_tiles(counts, n_tok * TOP_K, EXPERT_TILE)
    dest = destination_rows(off_pad, eid, pos)
    token_sorted = (jnp.argsort(dest.reshape(-1)) % n_tok).astype(I32)
    token_of_row = jnp.where(src >= 0, token_sorted[jnp.maximum(src, 0)], 0)[None, :]
    yp = routed_experts(hp, token_of_row, tile_eid, n_used, w_exp_gate, w_exp_up, w_exp_down, layer)
    return combine_shared_residual(yp, dest, wt.T, hp, w_sh_gate, w_sh_up, w_sh_down, x, mod_rows, layer, 5)


def kernel(x, c, norm_mix, norm_ffn, w_ada, b_ada, w_in, hgrn_lower_bounds, hgrn_out_norm, w_proj_a, w_proj_b, w_out, w_router, router_bias, w_exp_gate, w_exp_up, w_exp_down, w_sh_gate, w_sh_up, w_sh_down, norm_final):
    nb, S, D = x.shape
    depth = w_in.shape[0]
    a_width = w_proj_a.shape[1]
    b_width = w_proj_b.shape[1]
    in_cols = w_in.shape[2]
    qkv_cols = 3 * a_width
    rest_cols = in_cols - qkv_cols
    col_qb = 0
    col_fb = col_qb + b_width
    col_ib = col_fb + b_width
    col_gb = col_ib + b_width
    col_gate_a = col_gb + b_width
    col_gate_b = col_gate_a + D
    dilations = tuple(d for _, d in DILATED_PATTERNS)
    assert all(w // d == SPAN for w, d in DILATED_PATTERNS)

    lb_sm = jax.nn.softmax(hgrn_lower_bounds.astype(F32), axis=0)
    lb_all = jnp.cumsum(lb_sm, axis=0) - lb_sm[0:1]
    cos, sin = rope_tables(S)

    mod = adaln_modulation(c, w_ada, b_ada)
    mod_rows = mod.reshape(depth * nb * N_MOD, 1, D)

    for l in range(depth):
        h = norm_modulate(x, norm_mix[:, None, :], mod_rows, l, 0, 1)
        qkvs = qkv_projection(h, w_in, l, qkv_cols, 2 * a_width, cos, sin, dilations)
        rest = matmul_cols(h.reshape(nb * S, D), w_in, l, qkv_cols, rest_cols, BF16)

        outs, lses = [], []
        for qkv in qkvs:
            o_g, lse_g = dilated_attention_branch(qkv, a_width)
            outs.append(o_g)
            lses.append(jnp.transpose(lse_g, (0, 2, 1, 3)).reshape(nb, S, -1))
        o_a = attention_merge(outs, lses, dilations)

        o_b = hgrn2(rest.reshape(nb, S, rest_cols), col_qb, col_fb, col_ib, col_gb, b_width,
                    lb_all[l][None, :], hgrn_out_norm[l][None, :])

        u = merge_projection(o_a.reshape(nb * S, a_width), o_b.reshape(nb * S, b_width),
                             w_proj_a, w_proj_b, l, rest, col_gate_a, col_gate_b)
        x = projection_residual(u.reshape(nb, S, D), w_out, x, mod_rows, l, 2)

        x = moe_block(x, norm_ffn[:, None, :], mod_rows, l, w_router[l], router_bias[l],
                      w_exp_gate, w_exp_up, w_exp_down, w_sh_gate, w_sh_up, w_sh_down)
    return rmsnorm(x, norm_final)
```

```python
import functools

import jax
import jax.numpy as jnp
import numpy as np
from jax import lax
from jax.experimental import pallas as pl
from jax.experimental.pallas import tpu as pltpu

HEAD_DIM = 128
DILATED_PATTERNS = ((128, 1), (512, 4), (2048, 16))
SPAN = 128
ROPE_THETA = 10000.0
N_EXPERTS = 64
TOP_K = 8
N_GROUPS = 8
TOPK_GROUPS = 4
GROUP_SIZE = N_EXPERTS // N_GROUPS
ROUTED_SCALE = 2.5
N_MOD = 6
EPS = 1e-6

LANES = 128
SUBLANES = 8
VMEM_LIMIT_BYTES = 56 * 1024 * 1024

HGRN_CHUNK = 128
HGRN_LEVELS = 7

EXPERT_TILE = 512

BF16 = jnp.bfloat16
F32 = jnp.float32
U32 = jnp.uint32
I32 = jnp.int32

_NT = (((1,), (1,)), ((), ()))
_TN = (((0,), (0,)), ((), ()))


def _params(*sem):
    return pltpu.CompilerParams(dimension_semantics=sem, vmem_limit_bytes=VMEM_LIMIT_BYTES)


def _sigmoid(x):
    return 1.0 / (1.0 + jnp.exp(-x))


def _silu(x):
    return x * _sigmoid(x)


def _split_bf16(x):
    hi = x.astype(BF16)
    lo = (x - hi.astype(F32)).astype(BF16)
    return hi, lo


def _bf16_bits(x):
    return (lax.bitcast_convert_type(x, U32) + jnp.uint32(0x8000)) & jnp.uint32(0xFFFF0000)


def _pack_pair(lo, hi):
    return ((lax.bitcast_convert_type(lo, U32) + jnp.uint32(0x8000)) >> 16) | _bf16_bits(hi)


def _unpack_pair(p):
    lo = lax.bitcast_convert_type(p << 16, F32)
    hi = lax.bitcast_convert_type(p & jnp.uint32(0xFFFF0000), F32)
    return lo, hi


def _adaln_kernel(cb_ref, w_ref, bias_ref, out_ref, cs_ref, *, kc):
    K, bn = w_ref.shape
    nb = cb_ref.shape[0]

    @pl.when((pl.program_id(0) == 0) & (pl.program_id(1) == 0))
    def _():
        cs_ref[...] = _silu(cb_ref[...])

    for j in range(bn // LANES):
        cols = slice(j * LANES, (j + 1) * LANES)

        def body(i, accs):
            k0 = pl.multiple_of(i * kc, kc)
            w = w_ref[pl.ds(k0, kc), cols]
            out = []
            for b in range(nb):
                p = (w * cs_ref[b, pl.ds(k0, kc), :]).reshape(kc // SUBLANES, SUBLANES, LANES)
                out.append(accs[b] + jnp.sum(p, axis=0))
            return tuple(out)

        accs = lax.fori_loop(0, K // kc, body,
                             tuple(jnp.zeros((SUBLANES, LANES), F32) for _ in range(nb)))
        for b in range(nb):
            out_ref[b:b + 1, cols] = jnp.sum(accs[b], axis=0, keepdims=True) + bias_ref[:, cols]


def adaln_modulation(c, w_ada, b_ada):
    nl, K, N = w_ada.shape
    nb = c.shape[0]
    bn = min(512, N)
    kc = min(256, K)
    cb = jnp.broadcast_to(c[:, :, None], (nb, K, LANES))
    return pl.pallas_call(
        functools.partial(_adaln_kernel, kc=kc),
        out_shape=jax.ShapeDtypeStruct((nl, nb, N), F32),
        grid=(nl, N // bn),
        in_specs=[
            pl.BlockSpec((nb, K, LANES), lambda l, n: (0, 0, 0)),
            pl.BlockSpec((None, K, bn), lambda l, n: (l, 0, n)),
            pl.BlockSpec((None, 1, bn), lambda l, n: (l, 0, n)),
        ],
        out_specs=pl.BlockSpec((None, nb, bn), lambda l, n: (l, 0, n)),
        scratch_shapes=[pltpu.VMEM((nb, K, LANES), F32)],
        compiler_params=_params("arbitrary", "arbitrary"),
        name="adaln_modulation",
    )(cb, w_ada, b_ada.reshape(nl, 1, N))


def _norm_mod(x, w, shift, scale):
    y = x * lax.rsqrt(jnp.mean(x * x, axis=-1, keepdims=True) + EPS) * w
    return y * (1.0 + scale) + shift


def _norm_mod_kernel(x_ref, w_ref, shift_ref, scale_ref, out_ref):
    out_ref[...] = _norm_mod(x_ref[...], w_ref[...], shift_ref[...], scale_ref[...]).astype(out_ref.dtype)


def _mod_spec(D, which, layer, nb):
    return pl.BlockSpec((None, 1, D), lambda b, *_: ((layer * nb + b) * N_MOD + which, 0, 0))


def norm_modulate(x, norm_w, mod_rows, layer, which_shift, which_scale):
    nb, S, D = x.shape
    ts = min(512, S)
    return pl.pallas_call(
        _norm_mod_kernel,
        out_shape=jax.ShapeDtypeStruct((nb, S, D), BF16),
        grid=(nb, S // ts),
        in_specs=[
            pl.BlockSpec((None, ts, D), lambda b, s: (b, s, 0)),
            pl.BlockSpec((None, 1, D), lambda b, s: (layer, 0, 0)),
            _mod_spec(D, which_shift, layer, nb),
            _mod_spec(D, which_scale, layer, nb),
        ],
        out_specs=pl.BlockSpec((None, ts, D), lambda b, s: (b, s, 0)),
        compiler_params=_params("parallel", "parallel"),
        name="norm_modulate",
    )(x, norm_w, mod_rows, mod_rows)


def _qkv_proj_kernel(x_ref, w_ref, cos_ref, sin_ref, *refs, n_rope_tiles, dilations):
    out_refs, scr = refs[:-1], refs[-1]
    n_heads, bm, _ = scr.shape
    y = jnp.dot(x_ref[...], w_ref[...].astype(BF16), preferred_element_type=F32)
    n = pl.program_id(2)

    @pl.when(n < n_rope_tiles)
    def _():
        cos = cos_ref[...]
        sin = sin_ref[...]
        for h in range(n_heads):
            t = y[:, h * HEAD_DIM:(h + 1) * HEAD_DIM]
            scr[h] = t * cos + pltpu.roll(t, HEAD_DIM // 2, 1) * sin

    @pl.when(n >= n_rope_tiles)
    def _():
        for h in range(n_heads):
            scr[h] = y[:, h * HEAD_DIM:(h + 1) * HEAD_DIM]

    for d, o_ref in zip(dilations, out_refs):
        for r in range(d):
            for h in range(n_heads):
                o_ref[r, :, h * HEAD_DIM:(h + 1) * HEAD_DIM] = (
                    scr[h, pl.ds(r, bm // d, stride=d), :].astype(o_ref.dtype))


def rope_tables(S):
    half = HEAD_DIM // 2
    inv = ROPE_THETA ** (-jnp.arange(half, dtype=F32) / half)
    ang = jnp.arange(S, dtype=F32)[:, None] * inv[None, :]
    cos, sin = jnp.cos(ang), jnp.sin(ang)
    return jnp.concatenate([cos, cos], axis=-1), jnp.concatenate([-sin, sin], axis=-1)


def qkv_projection(h, w_in, layer, qkv_cols, rope_cols, cos, sin, dilations):
    nb, S, D = h.shape
    bm, bn = min(1024, S), min(512, qkv_cols)
    outs = tuple(jax.ShapeDtypeStruct((nb, d, S // d, qkv_cols), BF16) for d in dilations)
    out_specs = tuple(pl.BlockSpec((None, d, bm // d, bn), lambda b, m, n: (b, 0, m, n)) for d in dilations)
    return pl.pallas_call(
        functools.partial(_qkv_proj_kernel, n_rope_tiles=rope_cols // bn, dilations=dilations),
        out_shape=outs,
        grid=(nb, S // bm, qkv_cols // bn),
        in_specs=[pl.BlockSpec((None, bm, D), lambda b, m, n: (b, m, 0)),
                  pl.BlockSpec((None, D, bn), lambda b, m, n: (layer, 0, n)),
                  pl.BlockSpec((bm, HEAD_DIM), lambda b, m, n: (m, 0)),
                  pl.BlockSpec((bm, HEAD_DIM), lambda b, m, n: (m, 0))],
        out_specs=out_specs,
        scratch_shapes=[pltpu.VMEM((bn // HEAD_DIM, bm, HEAD_DIM), F32)],
        compiler_params=_params("parallel", "parallel", "arbitrary"),
        name="qkv_projection",
    )(h, w_in, cos, sin)


def _matmul_kernel(x_ref, w_ref, out_ref):
    out_ref[...] = jnp.dot(x_ref[...], w_ref[...].astype(BF16),
                           preferred_element_type=F32).astype(out_ref.dtype)


def matmul_cols(x, w, layer, col0, ncols, out_dtype):
    M, K = x.shape
    bm, bn = min(1024, M), min(512, ncols)
    return pl.pallas_call(
        _matmul_kernel,
        out_shape=jax.ShapeDtypeStruct((M, ncols), out_dtype),
        grid=(M // bm, ncols // bn),
        in_specs=[pl.BlockSpec((bm, K), lambda m, n: (m, 0)),
                  pl.BlockSpec((None, K, bn), lambda m, n: (layer, 0, col0 // bn + n))],
        out_specs=pl.BlockSpec((bm, bn), lambda m, n: (m, n)),
        compiler_params=_params("parallel", "arbitrary"),
        name="matmul_cols",
    )(x, w)


def _attn_kernel(q_ref, kp_ref, kc_ref, vp_ref, vc_ref, o_ref, lse_ref):
    i = pl.program_id(2)
    n_heads = q_ref.shape[-1] // HEAD_DIM
    qi = lax.broadcasted_iota(I32, (SPAN, 2 * SPAN), 0)
    kj = lax.broadcasted_iota(I32, (SPAN, 2 * SPAN), 1)
    mask = ((kj < SPAN) & (kj >= qi) & (i > 0)) | ((kj >= SPAN) & ((kj - SPAN) <= qi))
    scale = HEAD_DIM ** -0.5
    s = []
    for h in range(n_heads):
        cols = slice(h * HEAD_DIM, (h + 1) * HEAD_DIM)
        k_h = jnp.concatenate([kp_ref[:, cols], kc_ref[:, cols]], axis=0)
        s.append(lax.dot_general(q_ref[:, cols], k_h, _NT, preferred_element_type=F32))
    s = jnp.stack(s, axis=0)
    s = jnp.where(mask[None], s * scale, -jnp.inf)
    m = jnp.max(s, axis=-1, keepdims=True)
    p = jnp.exp(s - m).astype(BF16)
    ones = jnp.ones((2 * SPAN, HEAD_DIM), BF16)
    lses = []
    for h in range(n_heads):
        cols = slice(h * HEAD_DIM, (h + 1) * HEAD_DIM)
        v_h = jnp.concatenate([vp_ref[:, cols], vc_ref[:, cols]], axis=0)
        oe = jnp.dot(p[h], jnp.concatenate([v_h, ones], axis=1), preferred_element_type=F32)
        den = oe[:, HEAD_DIM:]
        o_ref[:, cols] = (oe[:, :HEAD_DIM] / den).astype(o_ref.dtype)
        lses.append(m[h] + jnp.log(den[:, :1]))
    lse_ref[...] = jnp.concatenate(lses, axis=-1)


def dilated_attention_branch(qkv, width):
    nb, d, L, _ = qkv.shape
    nblk = L // SPAN
    n_heads = width // HEAD_DIM

    def spec(col, prev):
        if prev:
            return pl.BlockSpec((None, None, SPAN, width), lambda b, r, i: (b, r, jnp.maximum(i - 1, 0), col))
        return pl.BlockSpec((None, None, SPAN, width), lambda b, r, i: (b, r, i, col))

    return pl.pallas_call(
        _attn_kernel,
        out_shape=(jax.ShapeDtypeStruct((nb, d, L, width), BF16),
                   jax.ShapeDtypeStruct((nb, d, L, n_heads), F32)),
        grid=(nb, d, nblk),
        in_specs=[spec(0, False), spec(1, True), spec(1, False), spec(2, True), spec(2, False)],
        out_specs=(pl.BlockSpec((None, None, SPAN, width), lambda b, r, i: (b, r, i, 0)),
                   pl.BlockSpec((None, None, SPAN, n_heads), lambda b, r, i: (b, r, i, 0))),
        compiler_params=_params("parallel", "parallel", "arbitrary"),
        name=f"dilated_attention_d{d}",
    )(qkv, qkv, qkv, qkv, qkv)


def _attn_merge_kernel(*refs, dilations):
    n = len(dilations)
    o_refs, lse_refs, out_ref, scr_refs = refs[:n], refs[n:2 * n], refs[2 * n], refs[2 * n + 1:]
    ts = out_ref.shape[0]
    nat = []
    for g, d in enumerate(dilations):
        if d == 1:
            nat.append(None)
            continue
        scr = scr_refs[len([x for x in nat if x is not None])]
        for r in range(d):
            for h in range(scr.shape[0]):
                scr[h, pl.ds(r, ts // d, stride=d), :] = (
                    o_refs[g][r, :, h * HEAD_DIM:(h + 1) * HEAD_DIM].astype(F32))
        nat.append(scr)
    lses = [r[...] for r in lse_refs]
    m = functools.reduce(jnp.maximum, lses)
    es = [jnp.exp(l - m) for l in lses]
    tot = functools.reduce(jnp.add, es)
    ws = [e / tot for e in es]
    for h in range(out_ref.shape[-1] // HEAD_DIM):
        cols = slice(h * HEAD_DIM, (h + 1) * HEAD_DIM)
        acc = jnp.zeros((ts, HEAD_DIM), F32)
        for g in range(n):
            o = o_refs[g][0, :, cols].astype(F32) if nat[g] is None else nat[g][h]
            acc = acc + ws[g][:, h:h + 1] * o
        out_ref[:, cols] = acc.astype(out_ref.dtype)


def attention_merge(outs, lses, dilations):
    nb, _, S, width = outs[dilations.index(1)].shape
    n_heads = lses[0].shape[-1]
    ts = min(512, S)
    o_specs = [pl.BlockSpec((None, d, ts // d, width), lambda b, s: (b, 0, s, 0)) for d in dilations]
    l_spec = pl.BlockSpec((None, ts, n_heads), lambda b, s: (b, s, 0))
    return pl.pallas_call(
        functools.partial(_attn_merge_kernel, dilations=dilations),
        out_shape=jax.ShapeDtypeStruct((nb, S, width), BF16),
        grid=(nb, S // ts),
        in_specs=o_specs + [l_spec] * len(lses),
        out_specs=pl.BlockSpec((None, ts, width), lambda b, s: (b, s, 0)),
        scratch_shapes=[pltpu.VMEM((n_heads, ts, HEAD_DIM), F32) for d in dilations if d > 1],
        compiler_params=_params("parallel", "parallel"),
        name="attention_merge",
    )(*outs, *lses)


def _hgrn_sum_matrix(C, levels):
    t = np.arange(C)[:, None]
    u = np.arange(C)[None, :]
    blocks = [(u <= t), (u > t)]
    for j in range(levels):
        half = C >> (j + 1)
        mid = (t // (2 * half)) * (2 * half) + half - 1
        upper = (t // half) % 2 == 1
        blocks.append(np.where(upper, (u > mid) & (u <= t), (u > t) & (u <= mid)))
    return np.concatenate(blocks, axis=0).astype(np.float32)


def _hgrn_kernel(q_ref, f_ref, i_ref, g_ref, lb_ref, nw_ref, sm_ref, out_ref, state_ref, *, n_chunks):
    C = HGRN_CHUNK

    @pl.when(pl.program_id(2) == 0)
    def _():
        state_ref[...] = jnp.zeros_like(state_ref)

    ti = lax.broadcasted_iota(I32, (C, C), 0)
    si = lax.broadcasted_iota(I32, (C, C), 1)
    xor = ti ^ si
    lower = si < ti
    lb = lb_ref[...]
    nw = nw_ref[...]
    state = state_ref[...]

    for c in range(n_chunks):
        rows = slice(c * C, (c + 1) * C)
        q = _silu(q_ref[rows, :].astype(F32))
        f = lb + (1.0 - lb) * _sigmoid(f_ref[rows, :].astype(F32))
        k = 1.0 - f
        g = jnp.log(f)
        v = i_ref[rows, :]
        g_hi, g_lo = _split_bf16(g)
        ghl = jnp.concatenate([g_hi, g_lo], axis=-1)
        e2 = jnp.dot(sm_ref[...], ghl, preferred_element_type=F32)
        e = e2[:, :HEAD_DIM] + e2[:, HEAD_DIM:]

        scores = jnp.where(ti == si, lax.dot_general(q.astype(BF16), k.astype(BF16), _NT,
                                                     preferred_element_type=F32), 0.0)
        for j in range(HGRN_LEVELS):
            a = jnp.exp(e[(2 + j) * C:(3 + j) * C, :])
            s_j = lax.dot_general((q * a).astype(BF16), (k * a).astype(BF16), _NT,
                                  preferred_element_type=F32)
            scores = scores + jnp.where(lower & ((xor >> (HGRN_LEVELS - 1 - j)) == 1), s_j, 0.0)

        o = jnp.dot(scores.astype(BF16), v, preferred_element_type=F32)
        decay = jnp.exp(e[0:C, :])
        o = o + lax.dot_general((q * decay).astype(BF16), state.astype(BF16), _NT,
                                preferred_element_type=F32)
        k_end = (k * jnp.exp(e[C:2 * C, :])).astype(BF16)
        state = state * decay[C - 1:C, :] + lax.dot_general(v, k_end, _TN, preferred_element_type=F32)

        y = o * lax.rsqrt(jnp.mean(o * o, axis=-1, keepdims=True) + EPS) * nw
        out_ref[rows, :] = (y * _silu(g_ref[rows, :].astype(F32))).astype(out_ref.dtype)
    state_ref[...] = state


def hgrn2(proj, col_q, col_f, col_i, col_g, width, lb, out_norm):
    nb, S, _ = proj.shape
    n_heads = width // HEAD_DIM
    C = HGRN_CHUNK
    ts = min(512, S)
    sm = jnp.asarray(_hgrn_sum_matrix(C, HGRN_LEVELS), dtype=BF16)

    def col_spec(col0):
        blk = col0 // HEAD_DIM
        return pl.BlockSpec((None, ts, HEAD_DIM), lambda b, h, s: (b, s, blk + h))

    return pl.pallas_call(
        functools.partial(_hgrn_kernel, n_chunks=ts // C),
        out_shape=jax.ShapeDtypeStruct((nb, S, width), BF16),
        grid=(nb, n_heads, S // ts),
        in_specs=[col_spec(col_q), col_spec(col_f), col_spec(col_i), col_spec(col_g),
                  pl.BlockSpec((1, HEAD_DIM), lambda b, h, s: (0, h)),
                  pl.BlockSpec((1, HEAD_DIM), lambda b, h, s: (0, 0)),
                  pl.BlockSpec(sm.shape, lambda b, h, s: (0, 0))],
        out_specs=pl.BlockSpec((None, ts, HEAD_DIM), lambda b, h, s: (b, s, h)),
        scratch_shapes=[pltpu.VMEM((HEAD_DIM, HEAD_DIM), F32)],
        compiler_params=_params("parallel", "parallel", "arbitrary"),
        name="hgrn2",
    )(proj, proj, proj, proj, lb, out_norm, sm)


def _merge_proj_kernel(oa_ref, ob_ref, wa_ref, wb_ref, ga_ref, gb_ref, out_ref):
    ya = jnp.dot(oa_ref[...], wa_ref[...].astype(BF16), preferred_element_type=F32)
    yb = jnp.dot(ob_ref[...], wb_ref[...].astype(BF16), preferred_element_type=F32)
    u = _sigmoid(ga_ref[...].astype(F32)) * ya + _sigmoid(gb_ref[...].astype(F32)) * yb
    out_ref[...] = u.astype(out_ref.dtype)


def merge_projection(o_a, o_b, w_a, w_b, layer, proj, col_ga, col_gb):
    M, K = o_a.shape
    N = w_a.shape[2]
    bm, bn = min(1024, M), min(512, N)
    return pl.pallas_call(
        _merge_proj_kernel,
        out_shape=jax.ShapeDtypeStruct((M, N), BF16),
        grid=(M // bm, N // bn),
        in_specs=[pl.BlockSpec((bm, K), lambda m, n: (m, 0)),
                  pl.BlockSpec((bm, K), lambda m, n: (m, 0)),
                  pl.BlockSpec((None, K, bn), lambda m, n: (layer, 0, n)),
                  pl.BlockSpec((None, K, bn), lambda m, n: (layer, 0, n)),
                  pl.BlockSpec((bm, bn), lambda m, n: (m, col_ga // bn + n)),
                  pl.BlockSpec((bm, bn), lambda m, n: (m, col_gb // bn + n))],
        out_specs=pl.BlockSpec((bm, bn), lambda m, n: (m, n)),
        compiler_params=_params("parallel", "arbitrary"),
        name="merge_projection",
    )(o_a, o_b, w_a, w_b, proj, proj)


def _proj_residual_kernel(u_ref, w_ref, x_ref, gate_ref, out_ref):
    y = jnp.dot(u_ref[...], w_ref[...].astype(BF16), preferred_element_type=F32)
    out_ref[...] = x_ref[...] + gate_ref[...] * y


def projection_residual(u, w, x, mod_rows, layer, which_gate):
    nb, S, K = u.shape
    D = w.shape[2]
    bm, bn = min(1024, S), min(512, D)
    return pl.pallas_call(
        _proj_residual_kernel,
        out_shape=jax.ShapeDtypeStruct((nb, S, D), F32),
        grid=(nb, S // bm, D // bn),
        in_specs=[pl.BlockSpec((None, bm, K), lambda b, m, n: (b, m, 0)),
                  pl.BlockSpec((None, K, bn), lambda b, m, n: (layer, 0, n)),
                  pl.BlockSpec((None, bm, bn), lambda b, m, n: (b, m, n)),
                  pl.BlockSpec((None, 1, bn), lambda b, m, n: ((layer * nb + b) * N_MOD + which_gate, 0, n))],
        out_specs=pl.BlockSpec((None, bm, bn), lambda b, m, n: (b, m, n)),
        compiler_params=_params("parallel", "parallel", "arbitrary"),
        name="projection_residual",
    )(u, w, x, mod_rows)


def _norm_router_kernel(x_ref, w_ref, shift_ref, scale_ref, wr_hi_ref, wr_lo_ref, hp_ref, logit_ref):
    h = _norm_mod(x_ref[...], w_ref[...], shift_ref[...], scale_ref[...])
    half = h.shape[1] // 2
    bits = _bf16_bits(h)
    hp_ref[...] = (bits[:, :half] >> 16) | bits[:, half:]
    h_r = lax.bitcast_convert_type(bits, F32)
    h_hi = h_r.astype(BF16)
    h_lo = (h - h_r).astype(BF16)
    logit_ref[...] = (lax.dot_general(wr_hi_ref[...], h_hi, _NT, preferred_element_type=F32)
                      + lax.dot_general(wr_hi_ref[...], h_lo, _NT, preferred_element_type=F32)
                      + lax.dot_general(wr_lo_ref[...], h_hi, _NT, preferred_element_type=F32))


def norm_router(x, norm_w, mod_rows, layer, w_router):
    nb, S, D = x.shape
    E = w_router.shape[1]
    ts = min(512, S)
    nts = S // ts
    wr_hi, wr_lo = _split_bf16(w_router.T)
    return pl.pallas_call(
        _norm_router_kernel,
        out_shape=(jax.ShapeDtypeStruct((nb * S, D // 2), U32), jax.ShapeDtypeStruct((E, nb * S), F32)),
        grid=(nb, nts),
        in_specs=[pl.BlockSpec((None, ts, D), lambda b, s: (b, s, 0)),
                  pl.BlockSpec((None, 1, D), lambda b, s: (layer, 0, 0)),
                  _mod_spec(D, 3, layer, nb),
                  _mod_spec(D, 4, layer, nb),
                  pl.BlockSpec((E, D), lambda b, s: (0, 0)),
                  pl.BlockSpec((E, D), lambda b, s: (0, 0))],
        out_specs=(pl.BlockSpec((ts, D // 2), lambda b, s: (b * nts + s, 0)),
                   pl.BlockSpec((E, ts), lambda b, s: (0, b * nts + s))),
        compiler_params=_params("parallel", "parallel"),
        name="norm_router",
    )(x, norm_w, mod_rows, mod_rows, wr_hi, wr_lo)


def _rank_lt(vals, n_rows, limit):
    ridx = lax.broadcasted_iota(I32, vals.shape, 0)
    cnt = jnp.zeros(vals.shape, I32)
    for r in range(n_rows):
        other = vals[r:r + 1, :]
        beats = (other > vals) | ((other == vals) & (r < ridx))
        cnt = cnt + beats.astype(I32)
    return cnt < limit


def _route_kernel(logit_ref, bias_ref, trie_ref, trit_ref, eid_ref, pos_ref, wt_ref, cnt_ref, carry_ref):
    @pl.when(pl.program_id(0) == 0)
    def _():
        carry_ref[...] = jnp.zeros_like(carry_ref)

    scores = _sigmoid(logit_ref[...])
    sel = scores + bias_ref[...]
    E, T = sel.shape
    sub = lax.broadcasted_iota(I32, (GROUP_SIZE, T), 0)
    gscores = []
    for g in range(N_GROUPS):
        v = sel[g * GROUP_SIZE:(g + 1) * GROUP_SIZE, :]
        m1 = jnp.max(v, axis=0, keepdims=True)
        first = jnp.min(jnp.where(v == m1, sub, GROUP_SIZE), axis=0, keepdims=True)
        m2 = jnp.max(jnp.where(sub == first, -jnp.inf, v), axis=0, keepdims=True)
        gscores.append(m1 + m2)
    gsc = jnp.concatenate(gscores, axis=0)
    gkeep = _rank_lt(gsc, N_GROUPS, TOPK_GROUPS)
    ekeep = jnp.concatenate(
        [jnp.broadcast_to(gkeep[g:g + 1, :], (GROUP_SIZE, T)) for g in range(N_GROUPS)], axis=0)
    masked = jnp.where(ekeep, sel, -jnp.inf)
    chosen = _rank_lt(masked, N_EXPERTS, TOP_K)
    w = jnp.where(chosen, scores, 0.0)
    combine = w / jnp.sum(w, axis=0, keepdims=True) * ROUTED_SCALE

    cf = jnp.where(chosen, 1.0, 0.0)
    cb = cf.astype(BF16)
    rank = jnp.dot(trie_ref[...], cb, preferred_element_type=F32)
    local = jnp.dot(cb, trit_ref[...], preferred_element_type=F32)
    carry = carry_ref[:, :1]
    pos = carry + local
    eidx = lax.broadcasted_iota(I32, (E, T), 0).astype(F32)
    eids, poss, wts = [], [], []
    for k in range(TOP_K):
        sel_k = chosen & (rank == k)
        eids.append(jnp.sum(jnp.where(sel_k, eidx, 0.0), axis=0, keepdims=True))
        poss.append(jnp.sum(jnp.where(sel_k, pos, 0.0), axis=0, keepdims=True))
        wts.append(jnp.sum(jnp.where(sel_k, combine, 0.0), axis=0, keepdims=True))
    eid_ref[...] = jnp.concatenate(eids, axis=0).astype(I32)
    pos_ref[...] = jnp.concatenate(poss, axis=0).astype(I32)
    wt_ref[...] = jnp.concatenate(wts, axis=0)
    total = carry + jnp.sum(cf, axis=1, keepdims=True)
    carry_ref[...] = jnp.broadcast_to(total, carry_ref.shape)
    cnt_ref[...] = jnp.broadcast_to(total, cnt_ref.shape).astype(I32)


def route(logits, router_bias):
    E, T = logits.shape
    tt = min(512, T)
    trie = jnp.asarray(np.tril(np.ones((E, E), np.float32), -1), BF16)
    trit = jnp.asarray(np.triu(np.ones((tt, tt), np.float32), 1), BF16)
    kt = pl.BlockSpec((TOP_K, tt), lambda t: (0, t))
    return pl.pallas_call(
        _route_kernel,
        out_shape=(jax.ShapeDtypeStruct((TOP_K, T), I32), jax.ShapeDtypeStruct((TOP_K, T), I32),
                   jax.ShapeDtypeStruct((TOP_K, T), F32), jax.ShapeDtypeStruct((E, LANES), I32)),
        grid=(T // tt,),
        in_specs=[pl.BlockSpec((E, tt), lambda t: (0, t)),
                  pl.BlockSpec((E, 1), lambda t: (0, 0)),
                  pl.BlockSpec((E, E), lambda t: (0, 0)),
                  pl.BlockSpec((tt, tt), lambda t: (0, 0))],
        out_specs=(kt, kt, kt, pl.BlockSpec((E, LANES), lambda t: (0, 0))),
        scratch_shapes=[pltpu.VMEM((E, LANES), F32)],
        compiler_params=_params("arbitrary"),
        name="route",
    )(logits, router_bias.reshape(E, 1), trie, trit)


def _dest_kernel(off_ref, eid_ref, pos_ref, dest_ref):
    eid = eid_ref[...]
    acc = pos_ref[...]
    for e in range(N_EXPERTS):
        acc = acc + jnp.where(eid == e, off_ref[e], 0)
    dest_ref[...] = acc


def destination_rows(offsets, eid, pos):
    K, T = eid.shape
    tt = min(2048, T)
    kt = pl.BlockSpec((K, tt), lambda t, off: (0, t))
    return pl.pallas_call(
        _dest_kernel,
        out_shape=jax.ShapeDtypeStruct((K, T), I32),
        grid_spec=pltpu.PrefetchScalarGridSpec(num_scalar_prefetch=1, grid=(T // tt,),
                                               in_specs=[kt, kt], out_specs=kt),
        compiler_params=_params("parallel"),
        name="destination_rows",
    )(offsets, eid, pos)


def _new_expert(eid_ref):
    v = pl.program_id(0)
    return (v == 0) | (eid_ref[v] != eid_ref[jnp.maximum(v - 1, 0)])


def _write_rows(out_ref, vals, tile_ref, lo_ref, hi_ref, first_ref):
    v = pl.program_id(0)
    tm = out_ref.shape[0]
    row = tile_ref[v] * tm + lax.broadcasted_iota(I32, (tm, 1), 0)
    mine = (row >= lo_ref[v]) & (row < hi_ref[v])

    @pl.when(first_ref[v] == 1)
    def _():
        out_ref[...] = jnp.where(mine, vals, jnp.zeros_like(vals))

    @pl.when(first_ref[v] == 0)
    def _():
        out_ref[...] = jnp.where(mine, vals, out_ref[...])


def _experts_up_kernel(tile_ref, eid_ref, lo_ref, hi_ref, first_ref, tok_ref, nxt_ref, hp_hbm, wg_ref, wu_ref,
                       act_ref, xbuf, wg_s, wu_s, sem, *, n_tiles):
    @pl.when(_new_expert(eid_ref))
    def _():
        wg_s[...] = wg_ref[...].astype(BF16)
        wu_s[...] = wu_ref[...].astype(BF16)

    v = pl.program_id(0)
    tm, half = xbuf.shape[1:]
    tile = tile_ref[v]
    slot = tile % 2

    def row_copy(idx_ref, i, s):
        return pltpu.make_async_copy(hp_hbm.at[pl.ds(idx_ref[0, i], 1)], xbuf.at[s, pl.ds(i, 1)], sem.at[s])

    def gate_up(x):
        x_lo, x_hi = _unpack_pair(x)
        x_lo, x_hi = x_lo.astype(BF16), x_hi.astype(BF16)
        gate = (jnp.dot(x_lo, wg_s[:half, :], preferred_element_type=F32)
                + jnp.dot(x_hi, wg_s[half:, :], preferred_element_type=F32))
        up = (jnp.dot(x_lo, wu_s[:half, :], preferred_element_type=F32)
              + jnp.dot(x_hi, wu_s[half:, :], preferred_element_type=F32))
        return _silu(gate) * up

    nonempty = hi_ref[v] > lo_ref[v]
    row = tile * tm + lax.broadcasted_iota(I32, (tm, 1), 0)
    mine = (row >= lo_ref[v]) & (row < hi_ref[v])

    def first_visit(request_next):
        @pl.when(tile == 0)
        def _():
            def start(i, carry):
                row_copy(tok_ref, i, 0).start()
                return carry
            lax.fori_loop(0, tm, start, 0, unroll=8)

        def wait(i, carry):
            row_copy(tok_ref, i, slot).wait()
            return carry
        lax.fori_loop(0, tm, wait, 0, unroll=8)

        if request_next:
            for i in range(tm):
                row_copy(nxt_ref, i, 1 - slot).start(priority=i % 2)
        vals = gate_up(xbuf[slot])
        act_ref[...] = jnp.where(mine, vals, 0.0)

    is_first = nonempty & (first_ref[v] == 1)

    @pl.when(is_first & (tile + 1 < n_tiles))
    def _():
        first_visit(True)

    @pl.when(is_first & (tile + 1 >= n_tiles))
    def _():
        first_visit(False)

    @pl.when(nonempty & (first_ref[v] == 0))
    def _():
        vals = gate_up(xbuf[slot])
        act_ref[...] = jnp.where(mine, vals, act_ref[...])


def _experts_down_kernel(tile_ref, eid_ref, lo_ref, hi_ref, first_ref, act_ref, wd_ref, y_ref, wd_s):
    @pl.when(_new_expert(eid_ref))
    def _():
        wd_s[...] = wd_ref[...].astype(BF16)

    v = pl.program_id(0)

    @pl.when(hi_ref[v] > lo_ref[v])
    def _():
        half = y_ref.shape[1]
        act = act_ref[...].astype(BF16)
        packed = _pack_pair(jnp.dot(act, wd_s[:, :half], preferred_element_type=F32),
                            jnp.dot(act, wd_s[:, half:], preferred_element_type=F32))
        _write_rows(y_ref, packed, tile_ref, lo_ref, hi_ref, first_ref)


def expert_segments(counts, n_rows, tm):
    E = counts.shape[0]
    n_tiles = n_rows // tm
    ends = jnp.cumsum(counts)
    starts = ends - counts
    cuts = jnp.sort(jnp.concatenate([jnp.arange(n_tiles, dtype=I32) * tm, starts.astype(I32)]))
    lo = cuts
    hi = jnp.concatenate([cuts[1:], jnp.array([n_rows], I32)])
    tile = jnp.minimum(lo // tm, n_tiles - 1)
    eid = jnp.minimum(jnp.sum(ends[None, :] <= lo[:, None], axis=1), E - 1).astype(I32)
    first = ((lo % tm == 0) & (hi > lo)).astype(I32)
    return tile.astype(I32), eid, lo.astype(I32), hi.astype(I32), first


def routed_experts(hp, token_of_row, segments, w_gate, w_up, w_down, layer):
    _, half = hp.shape
    R = token_of_row.shape[1]
    _, E, D, F = w_gate.shape
    tm = min(EXPERT_TILE, R)
    n_tiles = R // tm
    n_visits = segments[0].shape[0]

    def tile_spec(width):
        return pl.BlockSpec((tm, width), lambda v, tile, eid, lo, hi, first: (tile[v], 0))

    def weight_spec(rows, cols):
        return pl.BlockSpec((None, None, rows, cols), lambda v, tile, eid, lo, hi, first: (layer, eid[v], 0, 0))

    act = pl.pallas_call(
        functools.partial(_experts_up_kernel, n_tiles=n_tiles),
        out_shape=jax.ShapeDtypeStruct((R, F), F32),
        grid_spec=pltpu.PrefetchScalarGridSpec(
            num_scalar_prefetch=5, grid=(n_visits,),
            in_specs=[pl.BlockSpec((1, tm), lambda v, tile, eid, lo, hi, first: (0, tile[v]),
                                   memory_space=pltpu.SMEM),
                      pl.BlockSpec((1, tm), lambda v, tile, eid, lo, hi, first:
                                   (0, jnp.minimum(tile[v] + 1, n_tiles - 1)), memory_space=pltpu.SMEM),
                      pl.BlockSpec(memory_space=pl.ANY),
                      weight_spec(D, F), weight_spec(D, F)],
            out_specs=tile_spec(F),
            scratch_shapes=[pltpu.VMEM((2, tm, half), U32),
                            pltpu.VMEM((D, F), BF16), pltpu.VMEM((D, F), BF16),
                            pltpu.SemaphoreType.DMA((2,))]),
        compiler_params=_params("arbitrary"),
        name="routed_experts_up",
    )(*segments, token_of_row, token_of_row, hp, w_gate, w_up)
    return pl.pallas_call(
        _experts_down_kernel,
        out_shape=jax.ShapeDtypeStruct((R, half), U32),
        grid_spec=pltpu.PrefetchScalarGridSpec(
            num_scalar_prefetch=5, grid=(n_visits,),
            in_specs=[tile_spec(F), weight_spec(F, D)],
            out_specs=tile_spec(half),
            scratch_shapes=[pltpu.VMEM((F, D), BF16)]),
        compiler_params=_params("arbitrary"),
        name="routed_experts_down",
    )(*segments, act, w_down)


def _combine_kernel(dest_ref, hp_ref, wt_ref, sg_ref, su_ref, sd_ref, x_ref, gate_ref, yp_hbm, out_ref,
                    buf, sg_s, su_s, sd_s, sem, *, tt):
    @pl.when(pl.program_id(0) == 0)
    def _():
        sg_s[...] = sg_ref[...].astype(BF16)
        su_s[...] = su_ref[...].astype(BF16)
        sd_s[...] = sd_ref[...].astype(BF16)

    def row_copy(i, k):
        return pltpu.make_async_copy(yp_hbm.at[pl.ds(dest_ref[k, i], 1)], buf.at[k, pl.ds(i, 1)], sem)

    def start(i, carry):
        for k in range(TOP_K):
            row_copy(i, k).start(priority=k % 2)
        return carry

    def wait(i, carry):
        for k in range(TOP_K):
            row_copy(i, k).wait()
        return carry

    lax.fori_loop(0, tt, start, 0)

    half = hp_ref.shape[1]
    h_lo, h_hi = _unpack_pair(hp_ref[...])
    h_lo, h_hi = h_lo.astype(BF16), h_hi.astype(BF16)
    g = (jnp.dot(h_lo, sg_s[:half, :], preferred_element_type=F32)
         + jnp.dot(h_hi, sg_s[half:, :], preferred_element_type=F32))
    u = (jnp.dot(h_lo, su_s[:half, :], preferred_element_type=F32)
         + jnp.dot(h_hi, su_s[half:, :], preferred_element_type=F32))
    act = (_silu(g) * u).astype(BF16)
    y_lo = jnp.dot(act, sd_s[:, :half], preferred_element_type=F32)
    y_hi = jnp.dot(act, sd_s[:, half:], preferred_element_type=F32)

    lax.fori_loop(0, tt, wait, 0)

    for k in range(TOP_K):
        e_lo, e_hi = _unpack_pair(buf[k])
        wk = wt_ref[:, k:k + 1]
        y_lo = y_lo + wk * e_lo
        y_hi = y_hi + wk * e_hi
    gate = gate_ref[...]
    out_ref[:, :half] = x_ref[:, :half] + gate[:, :half] * y_lo
    out_ref[:, half:] = x_ref[:, half:] + gate[:, half:] * y_hi


def combine_shared_residual(yp, dest, wt, hp, w_sg, w_su, w_sd, x, mod_rows, layer, which_gate):
    nb, S, D = x.shape
    T, half = hp.shape
    K = dest.shape[0]
    F = w_sg.shape[2]
    tt = min(128, S)
    nts = S // tt
    xrow = pl.BlockSpec((None, tt, D), lambda t: (t // nts, t % nts, 0))
    return pl.pallas_call(
        functools.partial(_combine_kernel, tt=tt),
        out_shape=jax.ShapeDtypeStruct((nb, S, D), F32),
        grid=(T // tt,),
        in_specs=[pl.BlockSpec((K, tt), lambda t: (0, t), memory_space=pltpu.SMEM),
                  pl.BlockSpec((tt, half), lambda t: (t, 0)),
                  pl.BlockSpec((tt, K), lambda t: (t, 0)),
                  pl.BlockSpec((None, D, F), lambda t: (layer, 0, 0)),
                  pl.BlockSpec((None, D, F), lambda t: (layer, 0, 0)),
                  pl.BlockSpec((None, F, D), lambda t: (layer, 0, 0)),
                  xrow,
                  pl.BlockSpec((None, 1, D), lambda t: ((layer * nb + t // nts) * N_MOD + which_gate, 0, 0)),
                  pl.BlockSpec(memory_space=pl.ANY)],
        out_specs=xrow,
        scratch_shapes=[pltpu.VMEM((K, tt, half), U32),
                        pltpu.VMEM((D, F), BF16), pltpu.VMEM((D, F), BF16), pltpu.VMEM((F, D), BF16),
                        pltpu.SemaphoreType.DMA(())],
        compiler_params=_params("arbitrary"),
        name="combine_shared_residual",
    )(dest, hp, wt, w_sg, w_su, w_sd, x, mod_rows, yp)


def _rmsnorm_kernel(x_ref, w_ref, out_ref):
    x = x_ref[...]
    out_ref[...] = x * lax.rsqrt(jnp.mean(x * x, axis=-1, keepdims=True) + EPS) * w_ref[...]


def rmsnorm(x, w):
    nb, S, D = x.shape
    ts = min(512, S)
    return pl.pallas_call(
        _rmsnorm_kernel,
        out_shape=jax.ShapeDtypeStruct((nb, S, D), F32),
        grid=(nb, S // ts),
        in_specs=[pl.BlockSpec((None, ts, D), lambda b, s: (b, s, 0)),
                  pl.BlockSpec((1, D), lambda b, s: (0, 0))],
        out_specs=pl.BlockSpec((None, ts, D), lambda b, s: (b, s, 0)),
        compiler_params=_params("parallel", "parallel"),
        name="final_rmsnorm",
    )(x, w.reshape(1, D))


def moe_block(x, norm_w, mod_rows, layer, w_router, router_bias, w_exp_gate, w_exp_up, w_exp_down,
              w_sh_gate, w_sh_up, w_sh_down):
    hp, logits = norm_router(x, norm_w, mod_rows, layer, w_router)
    eid, pos, wt, counts = route(logits, router_bias)
    counts = counts[:, 0]
    offsets = jnp.cumsum(counts) - counts
    dest = destination_rows(offsets, eid, pos)
    n_tok = hp.shape[0]
    token_of_row = (jnp.argsort(dest.reshape(-1)) % n_tok).astype(I32)[None, :]
    n_rows = token_of_row.shape[1]
    segments = expert_segments(counts, n_rows, min(EXPERT_TILE, n_rows))
    yp = routed_experts(hp, token_of_row, segments, w_exp_gate, w_exp_up, w_exp_down, layer)
    return combine_shared_residual(yp, dest, wt.T, hp, w_sh_gate, w_sh_up, w_sh_down, x, mod_rows, layer, 5)


def kernel(x, c, norm_mix, norm_ffn, w_ada, b_ada, w_in, hgrn_lower_bounds, hgrn_out_norm, w_proj_a, w_proj_b, w_out, w_router, router_bias, w_exp_gate, w_exp_up, w_exp_down, w_sh_gate, w_sh_up, w_sh_down, norm_final):
    nb, S, D = x.shape
    depth = w_in.shape[0]
    a_width = w_proj_a.shape[1]
    b_width = w_proj_b.shape[1]
    in_cols = w_in.shape[2]
    qkv_cols = 3 * a_width
    rest_cols = in_cols - qkv_cols
    col_qb = 0
    col_fb = col_qb + b_width
    col_ib = col_fb + b_width
    col_gb = col_ib + b_width
    col_gate_a = col_gb + b_width
    col_gate_b = col_gate_a + D
    dilations = tuple(d for _, d in DILATED_PATTERNS)
    assert all(w // d == SPAN for w, d in DILATED_PATTERNS)

    lb_sm = jax.nn.softmax(hgrn_lower_bounds.astype(F32), axis=0)
    lb_all = jnp.cumsum(lb_sm, axis=0) - lb_sm[0:1]
    cos, sin = rope_tables(S)

    mod = adaln_modulation(c, w_ada, b_ada)
    mod_rows = mod.reshape(depth * nb * N_MOD, 1, D)

    for l in range(depth):
        h = norm_modulate(x, norm_mix[:, None, :], mod_rows, l, 0, 1)
        qkvs = qkv_projection(h, w_in, l, qkv_cols, 2 * a_width, cos, sin, dilations)
        rest = matmul_cols(h.reshape(nb * S, D), w_in, l, qkv_cols, rest_cols, BF16)

        outs, lses = [], []
        for qkv in qkvs:
            o_g, lse_g = dilated_attention_branch(qkv, a_width)
            outs.append(o_g)
            lses.append(jnp.transpose(lse_g, (0, 2, 1, 3)).reshape(nb, S, -1))
        o_a = attention_merge(outs, lses, dilations)

        o_b = hgrn2(rest.reshape(nb, S, rest_cols), col_qb, col_fb, col_ib, col_gb, b_width,
                    lb_all[l][None, :], hgrn_out_norm[l][None, :])

        u = merge_projection(o_a.reshape(nb * S, a_width), o_b.reshape(nb * S, b_width),
                             w_proj_a, w_proj_b, l, rest, col_gate_a, col_gate_b)
        x = projection_residual(u.reshape(nb, S, D), w_out, x, mod_rows, l, 2)

        x = moe_block(x, norm_ffn[:, None, :], mod_rows, l, w_router[l], router_bias[l],
                      w_exp_gate, w_exp_up, w_exp_down, w_sh_gate, w_sh_up, w_sh_down)
    return rmsnorm(x, norm_final)
```

```python
import functools

import jax
import jax.numpy as jnp
import numpy as np
from jax import lax
from jax.experimental import pallas as pl
from jax.experimental.pallas import tpu as pltpu

HEAD_DIM = 128
DILATED_PATTERNS = ((128, 1), (512, 4), (2048, 16))
SPAN = 128
ROPE_THETA = 10000.0
N_EXPERTS = 64
TOP_K = 8
N_GROUPS = 8
TOPK_GROUPS = 4
GROUP_SIZE = N_EXPERTS // N_GROUPS
ROUTED_SCALE = 2.5
N_MOD = 6
EPS = 1e-6

LANES = 128
SUBLANES = 8
VMEM_LIMIT_BYTES = 56 * 1024 * 1024

HGRN_CHUNK = 128
HGRN_LEVELS = 7

EXPERT_TILE = 512

BF16 = jnp.bfloat16
F32 = jnp.float32
U32 = jnp.uint32
I32 = jnp.int32

_NT = (((1,), (1,)), ((), ()))
_TN = (((0,), (0,)), ((), ()))


def _params(*sem):
    return pltpu.CompilerParams(dimension_semantics=sem, vmem_limit_bytes=VMEM_LIMIT_BYTES)


def _sigmoid(x):
    return 1.0 / (1.0 + jnp.exp(-x))


def _silu(x):
    return x * _sigmoid(x)


def _split_bf16(x):
    hi = x.astype(BF16)
    lo = (x - hi.astype(F32)).astype(BF16)
    return hi, lo


def _bf16_bits(x):
    return (lax.bitcast_convert_type(x, U32) + jnp.uint32(0x8000)) & jnp.uint32(0xFFFF0000)


def _pack_pair(lo, hi):
    return ((lax.bitcast_convert_type(lo, U32) + jnp.uint32(0x8000)) >> 16) | _bf16_bits(hi)


def _unpack_pair(p):
    lo = lax.bitcast_convert_type(p << 16, F32)
    hi = lax.bitcast_convert_type(p & jnp.uint32(0xFFFF0000), F32)
    return lo, hi


def _adaln_kernel(cb_ref, w_ref, bias_ref, out_ref, cs_ref, *, kc):
    K, bn = w_ref.shape
    nb = cb_ref.shape[0]

    @pl.when((pl.program_id(0) == 0) & (pl.program_id(1) == 0))
    def _():
        cs_ref[...] = _silu(cb_ref[...])

    for j in range(bn // LANES):
        cols = slice(j * LANES, (j + 1) * LANES)

        def body(i, accs):
            k0 = pl.multiple_of(i * kc, kc)
            w = w_ref[pl.ds(k0, kc), cols]
            out = []
            for b in range(nb):
                p = (w * cs_ref[b, pl.ds(k0, kc), :]).reshape(kc // SUBLANES, SUBLANES, LANES)
                out.append(accs[b] + jnp.sum(p, axis=0))
            return tuple(out)

        accs = lax.fori_loop(0, K // kc, body,
                             tuple(jnp.zeros((SUBLANES, LANES), F32) for _ in range(nb)))
        for b in range(nb):
            out_ref[b:b + 1, cols] = jnp.sum(accs[b], axis=0, keepdims=True) + bias_ref[:, cols]


def adaln_modulation(c, w_ada, b_ada):
    nl, K, N = w_ada.shape
    nb = c.shape[0]
    bn = min(512, N)
    kc = min(256, K)
    cb = jnp.broadcast_to(c[:, :, None], (nb, K, LANES))
    return pl.pallas_call(
        functools.partial(_adaln_kernel, kc=kc),
        out_shape=jax.ShapeDtypeStruct((nl, nb, N), F32),
        grid=(nl, N // bn),
        in_specs=[
            pl.BlockSpec((nb, K, LANES), lambda l, n: (0, 0, 0)),
            pl.BlockSpec((None, K, bn), lambda l, n: (l, 0, n)),
            pl.BlockSpec((None, 1, bn), lambda l, n: (l, 0, n)),
        ],
        out_specs=pl.BlockSpec((None, nb, bn), lambda l, n: (l, 0, n)),
        scratch_shapes=[pltpu.VMEM((nb, K, LANES), F32)],
        compiler_params=_params("arbitrary", "arbitrary"),
        name="adaln_modulation",
    )(cb, w_ada, b_ada.reshape(nl, 1, N))


def _norm_mod(x, w, shift, scale):
    y = x * lax.rsqrt(jnp.mean(x * x, axis=-1, keepdims=True) + EPS) * w
    return y * (1.0 + scale) + shift


def _norm_mod_kernel(x_ref, w_ref, shift_ref, scale_ref, out_ref):
    out_ref[...] = _norm_mod(x_ref[...], w_ref[...], shift_ref[...], scale_ref[...]).astype(out_ref.dtype)


def _mod_spec(D, which, layer, nb):
    return pl.BlockSpec((None, 1, D), lambda b, *_: ((layer * nb + b) * N_MOD + which, 0, 0))


def norm_modulate(x, norm_w, mod_rows, layer, which_shift, which_scale):
    nb, S, D = x.shape
    ts = min(512, S)
    return pl.pallas_call(
        _norm_mod_kernel,
        out_shape=jax.ShapeDtypeStruct((nb, S, D), BF16),
        grid=(nb, S // ts),
        in_specs=[
            pl.BlockSpec((None, ts, D), lambda b, s: (b, s, 0)),
            pl.BlockSpec((None, 1, D), lambda b, s: (layer, 0, 0)),
            _mod_spec(D, which_shift, layer, nb),
            _mod_spec(D, which_scale, layer, nb),
        ],
        out_specs=pl.BlockSpec((None, ts, D), lambda b, s: (b, s, 0)),
        compiler_params=_params("parallel", "parallel"),
        name="norm_modulate",
    )(x, norm_w, mod_rows, mod_rows)


def _qkv_proj_kernel(x_ref, w_ref, cos_ref, sin_ref, *refs, n_rope_tiles, dilations):
    out_refs, scr = refs[:-1], refs[-1]
    n_heads, bm, _ = scr.shape
    y = jnp.dot(x_ref[...], w_ref[...].astype(BF16), preferred_element_type=F32)
    n = pl.program_id(2)

    @pl.when(n < n_rope_tiles)
    def _():
        cos = cos_ref[...]
        sin = sin_ref[...]
        for h in range(n_heads):
            t = y[:, h * HEAD_DIM:(h + 1) * HEAD_DIM]
            scr[h] = t * cos + pltpu.roll(t, HEAD_DIM // 2, 1) * sin

    @pl.when(n >= n_rope_tiles)
    def _():
        for h in range(n_heads):
            scr[h] = y[:, h * HEAD_DIM:(h + 1) * HEAD_DIM]

    for d, o_ref in zip(dilations, out_refs):
        for r in range(d):
            for h in range(n_heads):
                o_ref[r, :, h * HEAD_DIM:(h + 1) * HEAD_DIM] = (
                    scr[h, pl.ds(r, bm // d, stride=d), :].astype(o_ref.dtype))


def rope_tables(S):
    half = HEAD_DIM // 2
    inv = ROPE_THETA ** (-jnp.arange(half, dtype=F32) / half)
    ang = jnp.arange(S, dtype=F32)[:, None] * inv[None, :]
    cos, sin = jnp.cos(ang), jnp.sin(ang)
    return jnp.concatenate([cos, cos], axis=-1), jnp.concatenate([-sin, sin], axis=-1)


def qkv_projection(h, w_in, layer, qkv_cols, rope_cols, cos, sin, dilations):
    nb, S, D = h.shape
    bm, bn = min(1024, S), min(512, qkv_cols)
    outs = tuple(jax.ShapeDtypeStruct((nb, d, S // d, qkv_cols), BF16) for d in dilations)
    out_specs = tuple(pl.BlockSpec((None, d, bm // d, bn), lambda b, m, n: (b, 0, m, n)) for d in dilations)
    return pl.pallas_call(
        functools.partial(_qkv_proj_kernel, n_rope_tiles=rope_cols // bn, dilations=dilations),
        out_shape=outs,
        grid=(nb, S // bm, qkv_cols // bn),
        in_specs=[pl.BlockSpec((None, bm, D), lambda b, m, n: (b, m, 0)),
                  pl.BlockSpec((None, D, bn), lambda b, m, n: (layer, 0, n)),
                  pl.BlockSpec((bm, HEAD_DIM), lambda b, m, n: (m, 0)),
                  pl.BlockSpec((bm, HEAD_DIM), lambda b, m, n: (m, 0))],
        out_specs=out_specs,
        scratch_shapes=[pltpu.VMEM((bn // HEAD_DIM, bm, HEAD_DIM), F32)],
        compiler_params=_params("parallel", "parallel", "arbitrary"),
        name="qkv_projection",
    )(h, w_in, cos, sin)


def _matmul_kernel(x_ref, w_ref, out_ref):
    out_ref[...] = jnp.dot(x_ref[...], w_ref[...].astype(BF16),
                           preferred_element_type=F32).astype(out_ref.dtype)


def matmul_cols(x, w, layer, col0, ncols, out_dtype):
    M, K = x.shape
    bm, bn = min(1024, M), min(512, ncols)
    return pl.pallas_call(
        _matmul_kernel,
        out_shape=jax.ShapeDtypeStruct((M, ncols), out_dtype),
        grid=(M // bm, ncols // bn),
        in_specs=[pl.BlockSpec((bm, K), lambda m, n: (m, 0)),
                  pl.BlockSpec((None, K, bn), lambda m, n: (layer, 0, col0 // bn + n))],
        out_specs=pl.BlockSpec((bm, bn), lambda m, n: (m, n)),
        compiler_params=_params("parallel", "arbitrary"),
        name="matmul_cols",
    )(x, w)


def _attn_kernel(q_ref, kp_ref, kc_ref, vp_ref, vc_ref, o_ref, lse_ref):
    i = pl.program_id(2)
    n_heads = q_ref.shape[-1] // HEAD_DIM
    qi = lax.broadcasted_iota(I32, (SPAN, 2 * SPAN), 0)
    kj = lax.broadcasted_iota(I32, (SPAN, 2 * SPAN), 1)
    mask = ((kj < SPAN) & (kj >= qi) & (i > 0)) | ((kj >= SPAN) & ((kj - SPAN) <= qi))
    scale = HEAD_DIM ** -0.5
    s = []
    for h in range(n_heads):
        cols = slice(h * HEAD_DIM, (h + 1) * HEAD_DIM)
        k_h = jnp.concatenate([kp_ref[:, cols], kc_ref[:, cols]], axis=0)
        s.append(lax.dot_general(q_ref[:, cols], k_h, _NT, preferred_element_type=F32))
    s = jnp.stack(s, axis=0)
    s = jnp.where(mask[None], s * scale, -jnp.inf)
    m = jnp.max(s, axis=-1, keepdims=True)
    p = jnp.exp(s - m).astype(BF16)
    ones = jnp.ones((2 * SPAN, HEAD_DIM), BF16)
    lses = []
    for h in range(n_heads):
        cols = slice(h * HEAD_DIM, (h + 1) * HEAD_DIM)
        v_h = jnp.concatenate([vp_ref[:, cols], vc_ref[:, cols]], axis=0)
        oe = jnp.dot(p[h], jnp.concatenate([v_h, ones], axis=1), preferred_element_type=F32)
        den = oe[:, HEAD_DIM:]
        o_ref[:, cols] = (oe[:, :HEAD_DIM] / den).astype(o_ref.dtype)
        lses.append(m[h] + jnp.log(den[:, :1]))
    lse_ref[...] = jnp.concatenate(lses, axis=-1)


def dilated_attention_branch(qkv, width):
    nb, d, L, _ = qkv.shape
    nblk = L // SPAN
    n_heads = width // HEAD_DIM

    def spec(col, prev):
        if prev:
            return pl.BlockSpec((None, None, SPAN, width), lambda b, r, i: (b, r, jnp.maximum(i - 1, 0), col))
        return pl.BlockSpec((None, None, SPAN, width), lambda b, r, i: (b, r, i, col))

    return pl.pallas_call(
        _attn_kernel,
        out_shape=(jax.ShapeDtypeStruct((nb, d, L, width), BF16),
                   jax.ShapeDtypeStruct((nb, d, L, n_heads), F32)),
        grid=(nb, d, nblk),
        in_specs=[spec(0, False), spec(1, True), spec(1, False), spec(2, True), spec(2, False)],
        out_specs=(pl.BlockSpec((None, None, SPAN, width), lambda b, r, i: (b, r, i, 0)),
                   pl.BlockSpec((None, None, SPAN, n_heads), lambda b, r, i: (b, r, i, 0))),
        compiler_params=_params("parallel", "parallel", "arbitrary"),
        name=f"dilated_attention_d{d}",
    )(qkv, qkv, qkv, qkv, qkv)


def _attn_merge_kernel(*refs, dilations):
    n = len(dilations)
    o_refs, lse_refs, out_ref, scr_refs = refs[:n], refs[n:2 * n], refs[2 * n], refs[2 * n + 1:]
    ts = out_ref.shape[0]
    nat = []
    for g, d in enumerate(dilations):
        if d == 1:
            nat.append(None)
            continue
        scr = scr_refs[len([x for x in nat if x is not None])]
        for r in range(d):
            for h in range(scr.shape[0]):
                scr[h, pl.ds(r, ts // d, stride=d), :] = (
                    o_refs[g][r, :, h * HEAD_DIM:(h + 1) * HEAD_DIM].astype(F32))
        nat.append(scr)
    lses = [r[...] for r in lse_refs]
    m = functools.reduce(jnp.maximum, lses)
    es = [jnp.exp(l - m) for l in lses]
    tot = functools.reduce(jnp.add, es)
    ws = [e / tot for e in es]
    for h in range(out_ref.shape[-1] // HEAD_DIM):
        cols = slice(h * HEAD_DIM, (h + 1) * HEAD_DIM)
        acc = jnp.zeros((ts, HEAD_DIM), F32)
        for g in range(n):
            o = o_refs[g][0, :, cols].astype(F32) if nat[g] is None else nat[g][h]
            acc = acc + ws[g][:, h:h + 1] * o
        out_ref[:, cols] = acc.astype(out_ref.dtype)


def attention_merge(outs, lses, dilations):
    nb, _, S, width = outs[dilations.index(1)].shape
    n_heads = lses[0].shape[-1]
    ts = min(512, S)
    o_specs = [pl.BlockSpec((None, d, ts // d, width), lambda b, s: (b, 0, s, 0)) for d in dilations]
    l_spec = pl.BlockSpec((None, ts, n_heads), lambda b, s: (b, s, 0))
    return pl.pallas_call(
        functools.partial(_attn_merge_kernel, dilations=dilations),
        out_shape=jax.ShapeDtypeStruct((nb, S, width), BF16),
        grid=(nb, S // ts),
        in_specs=o_specs + [l_spec] * len(lses),
        out_specs=pl.BlockSpec((None, ts, width), lambda b, s: (b, s, 0)),
        scratch_shapes=[pltpu.VMEM((n_heads, ts, HEAD_DIM), F32) for d in dilations if d > 1],
        compiler_params=_params("parallel", "parallel"),
        name="attention_merge",
    )(*outs, *lses)


def _hgrn_sum_matrix(C, levels):
    t = np.arange(C)[:, None]
    u = np.arange(C)[None, :]
    blocks = [(u <= t), (u > t)]
    for j in range(levels):
        half = C >> (j + 1)
        mid = (t // (2 * half)) * (2 * half) + half - 1
        upper = (t // half) % 2 == 1
        blocks.append(np.where(upper, (u > mid) & (u <= t), (u > t) & (u <= mid)))
    return np.concatenate(blocks, axis=0).astype(np.float32)


def _hgrn_kernel(q_ref, f_ref, i_ref, g_ref, lb_ref, nw_ref, sm_ref, out_ref, state_ref, *, n_chunks):
    C = HGRN_CHUNK

    @pl.when(pl.program_id(2) == 0)
    def _():
        state_ref[...] = jnp.zeros_like(state_ref)

    ti = lax.broadcasted_iota(I32, (C, C), 0)
    si = lax.broadcasted_iota(I32, (C, C), 1)
    xor = ti ^ si
    lower = si < ti
    nw = nw_ref[...]
    n_heads = out_ref.shape[-1] // HEAD_DIM
    states = [state_ref[h] for h in range(n_heads)]

    for c, h in [(c, h) for c in range(n_chunks) for h in range(n_heads)]:
        rows = slice(c * C, (c + 1) * C)
        cols = slice(h * HEAD_DIM, (h + 1) * HEAD_DIM)
        lb = lb_ref[:, cols]
        state = states[h]
        q = _silu(q_ref[rows, cols].astype(F32))
        f = lb + (1.0 - lb) * _sigmoid(f_ref[rows, cols].astype(F32))
        k = 1.0 - f
        g = jnp.log(f)
        v = i_ref[rows, cols]
        g_hi, g_lo = _split_bf16(g)
        ghl = jnp.concatenate([g_hi, g_lo], axis=-1)
        e2 = jnp.dot(sm_ref[...], ghl, preferred_element_type=F32)
        e = e2[:, :HEAD_DIM] + e2[:, HEAD_DIM:]

        scores = jnp.where(ti == si, lax.dot_general(q.astype(BF16), k.astype(BF16), _NT,
                                                     preferred_element_type=F32), 0.0)
        for j in range(HGRN_LEVELS):
            a = jnp.exp(e[(2 + j) * C:(3 + j) * C, :])
            s_j = lax.dot_general((q * a).astype(BF16), (k * a).astype(BF16), _NT,
                                  preferred_element_type=F32)
            scores = scores + jnp.where(lower & ((xor >> (HGRN_LEVELS - 1 - j)) == 1), s_j, 0.0)

        o = jnp.dot(scores.astype(BF16), v, preferred_element_type=F32)
        decay = jnp.exp(e[0:C, :])
        o = o + lax.dot_general((q * decay).astype(BF16), state.astype(BF16), _NT,
                                preferred_element_type=F32)
        k_end = (k * jnp.exp(e[C:2 * C, :])).astype(BF16)
        states[h] = state * decay[C - 1:C, :] + lax.dot_general(v, k_end, _TN, preferred_element_type=F32)

        y = o * lax.rsqrt(jnp.mean(o * o, axis=-1, keepdims=True) + EPS) * nw
        out_ref[rows, cols] = (y * _silu(g_ref[rows, cols].astype(F32))).astype(out_ref.dtype)
    for h in range(n_heads):
        state_ref[h] = states[h]


def hgrn2(proj, col_q, col_f, col_i, col_g, width, lb, out_norm):
    nb, S, _ = proj.shape
    n_heads = width // HEAD_DIM
    C = HGRN_CHUNK
    ts = min(512, S)
    sm = jnp.asarray(_hgrn_sum_matrix(C, HGRN_LEVELS), dtype=BF16)

    hps = 2 if n_heads % 2 == 0 else 1
    bw = hps * HEAD_DIM

    def col_spec(col0):
        blk = col0 // bw
        return pl.BlockSpec((None, ts, bw), lambda b, h, s: (b, s, blk + h))

    return pl.pallas_call(
        functools.partial(_hgrn_kernel, n_chunks=ts // C),
        out_shape=jax.ShapeDtypeStruct((nb, S, width), BF16),
        grid=(nb, n_heads // hps, S // ts),
        in_specs=[col_spec(col_q), col_spec(col_f), col_spec(col_i), col_spec(col_g),
                  pl.BlockSpec((1, bw), lambda b, h, s: (0, h)),
                  pl.BlockSpec((1, HEAD_DIM), lambda b, h, s: (0, 0)),
                  pl.BlockSpec(sm.shape, lambda b, h, s: (0, 0))],
        out_specs=pl.BlockSpec((None, ts, bw), lambda b, h, s: (b, s, h)),
        scratch_shapes=[pltpu.VMEM((hps, HEAD_DIM, HEAD_DIM), F32)],
        compiler_params=_params("parallel", "parallel", "arbitrary"),
        name="hgrn2",
    )(proj, proj, proj, proj, lb, out_norm, sm)


def _merge_proj_kernel(oa_ref, ob_ref, wa_ref, wb_ref, ga_ref, gb_ref, out_ref):
    ya = jnp.dot(oa_ref[...], wa_ref[...].astype(BF16), preferred_element_type=F32)
    yb = jnp.dot(ob_ref[...], wb_ref[...].astype(BF16), preferred_element_type=F32)
    u = _sigmoid(ga_ref[...].astype(F32)) * ya + _sigmoid(gb_ref[...].astype(F32)) * yb
    out_ref[...] = u.astype(out_ref.dtype)


def merge_projection(o_a, o_b, w_a, w_b, layer, proj, col_ga, col_gb):
    M, K = o_a.shape
    N = w_a.shape[2]
    bm, bn = min(1024, M), min(512, N)
    return pl.pallas_call(
        _merge_proj_kernel,
        out_shape=jax.ShapeDtypeStruct((M, N), BF16),
        grid=(M // bm, N // bn),
        in_specs=[pl.BlockSpec((bm, K), lambda m, n: (m, 0)),
                  pl.BlockSpec((bm, K), lambda m, n: (m, 0)),
                  pl.BlockSpec((None, K, bn), lambda m, n: (layer, 0, n)),
                  pl.BlockSpec((None, K, bn), lambda m, n: (layer, 0, n)),
                  pl.BlockSpec((bm, bn), lambda m, n: (m, col_ga // bn + n)),
                  pl.BlockSpec((bm, bn), lambda m, n: (m, col_gb // bn + n))],
        out_specs=pl.BlockSpec((bm, bn), lambda m, n: (m, n)),
        compiler_params=_params("parallel", "arbitrary"),
        name="merge_projection",
    )(o_a, o_b, w_a, w_b, proj, proj)


def _proj_residual_kernel(u_ref, w_ref, x_ref, gate_ref, out_ref):
    y = jnp.dot(u_ref[...], w_ref[...].astype(BF16), preferred_element_type=F32)
    out_ref[...] = x_ref[...] + gate_ref[...] * y


def projection_residual(u, w, x, mod_rows, layer, which_gate):
    nb, S, K = u.shape
    D = w.shape[2]
    bm, bn = min(1024, S), min(512, D)
    return pl.pallas_call(
        _proj_residual_kernel,
        out_shape=jax.ShapeDtypeStruct((nb, S, D), F32),
        grid=(nb, S // bm, D // bn),
        in_specs=[pl.BlockSpec((None, bm, K), lambda b, m, n: (b, m, 0)),
                  pl.BlockSpec((None, K, bn), lambda b, m, n: (layer, 0, n)),
                  pl.BlockSpec((None, bm, bn), lambda b, m, n: (b, m, n)),
                  pl.BlockSpec((None, 1, bn), lambda b, m, n: ((layer * nb + b) * N_MOD + which_gate, 0, n))],
        out_specs=pl.BlockSpec((None, bm, bn), lambda b, m, n: (b, m, n)),
        compiler_params=_params("parallel", "parallel", "arbitrary"),
        name="projection_residual",
    )(u, w, x, mod_rows)


def _norm_router_kernel(x_ref, w_ref, shift_ref, scale_ref, wr_hi_ref, wr_lo_ref, hp_ref, logit_ref):
    h = _norm_mod(x_ref[...], w_ref[...], shift_ref[...], scale_ref[...])
    half = h.shape[1] // 2
    bits = _bf16_bits(h)
    hp_ref[...] = (bits[:, :half] >> 16) | bits[:, half:]
    h_r = lax.bitcast_convert_type(bits, F32)
    h_hi = h_r.astype(BF16)
    h_lo = (h - h_r).astype(BF16)
    logit_ref[...] = (lax.dot_general(wr_hi_ref[...], h_hi, _NT, preferred_element_type=F32)
                      + lax.dot_general(wr_hi_ref[...], h_lo, _NT, preferred_element_type=F32)
                      + lax.dot_general(wr_lo_ref[...], h_hi, _NT, preferred_element_type=F32))


def norm_router(x, norm_w, mod_rows, layer, w_router):
    nb, S, D = x.shape
    E = w_router.shape[1]
    ts = min(512, S)
    nts = S // ts
    wr_hi, wr_lo = _split_bf16(w_router.T)
    return pl.pallas_call(
        _norm_router_kernel,
        out_shape=(jax.ShapeDtypeStruct((nb * S, D // 2), U32), jax.ShapeDtypeStruct((E, nb * S), F32)),
        grid=(nb, nts),
        in_specs=[pl.BlockSpec((None, ts, D), lambda b, s: (b, s, 0)),
                  pl.BlockSpec((None, 1, D), lambda b, s: (layer, 0, 0)),
                  _mod_spec(D, 3, layer, nb),
                  _mod_spec(D, 4, layer, nb),
                  pl.BlockSpec((E, D), lambda b, s: (0, 0)),
                  pl.BlockSpec((E, D), lambda b, s: (0, 0))],
        out_specs=(pl.BlockSpec((ts, D // 2), lambda b, s: (b * nts + s, 0)),
                   pl.BlockSpec((E, ts), lambda b, s: (0, b * nts + s))),
        compiler_params=_params("parallel", "parallel"),
        name="norm_router",
    )(x, norm_w, mod_rows, mod_rows, wr_hi, wr_lo)


def _rank_lt(vals, n_rows, limit):
    ridx = lax.broadcasted_iota(I32, vals.shape, 0)
    cnt = jnp.zeros(vals.shape, I32)
    for r in range(n_rows):
        other = vals[r:r + 1, :]
        beats = (other > vals) | ((other == vals) & (r < ridx))
        cnt = cnt + beats.astype(I32)
    return cnt < limit


def _route_kernel(logit_ref, bias_ref, trie_ref, trit_ref, eid_ref, pos_ref, wt_ref, cnt_ref, carry_ref):
    @pl.when(pl.program_id(0) == 0)
    def _():
        carry_ref[...] = jnp.zeros_like(carry_ref)

    scores = _sigmoid(logit_ref[...])
    sel = scores + bias_ref[...]
    E, T = sel.shape
    sub = lax.broadcasted_iota(I32, (GROUP_SIZE, T), 0)
    gscores = []
    for g in range(N_GROUPS):
        v = sel[g * GROUP_SIZE:(g + 1) * GROUP_SIZE, :]
        m1 = jnp.max(v, axis=0, keepdims=True)
        first = jnp.min(jnp.where(v == m1, sub, GROUP_SIZE), axis=0, keepdims=True)
        m2 = jnp.max(jnp.where(sub == first, -jnp.inf, v), axis=0, keepdims=True)
        gscores.append(m1 + m2)
    gsc = jnp.concatenate(gscores, axis=0)
    gkeep = _rank_lt(gsc, N_GROUPS, TOPK_GROUPS)
    ekeep = jnp.concatenate(
        [jnp.broadcast_to(gkeep[g:g + 1, :], (GROUP_SIZE, T)) for g in range(N_GROUPS)], axis=0)
    masked = jnp.where(ekeep, sel, -jnp.inf)
    chosen = _rank_lt(masked, N_EXPERTS, TOP_K)
    w = jnp.where(chosen, scores, 0.0)
    combine = w / jnp.sum(w, axis=0, keepdims=True) * ROUTED_SCALE

    cf = jnp.where(chosen, 1.0, 0.0)
    cb = cf.astype(BF16)
    rank = jnp.dot(trie_ref[...], cb, preferred_element_type=F32)
    local = jnp.dot(cb, trit_ref[...], preferred_element_type=F32)
    carry = carry_ref[:, :1]
    pos = carry + local
    eidx = lax.broadcasted_iota(I32, (E, T), 0).astype(F32)
    eids, poss, wts = [], [], []
    for k in range(TOP_K):
        sel_k = chosen & (rank == k)
        eids.append(jnp.sum(jnp.where(sel_k, eidx, 0.0), axis=0, keepdims=True))
        poss.append(jnp.sum(jnp.where(sel_k, pos, 0.0), axis=0, keepdims=True))
        wts.append(jnp.sum(jnp.where(sel_k, combine, 0.0), axis=0, keepdims=True))
    eid_ref[...] = jnp.concatenate(eids, axis=0).astype(I32)
    pos_ref[...] = jnp.concatenate(poss, axis=0).astype(I32)
    wt_ref[...] = jnp.concatenate(wts, axis=0)
    total = carry + jnp.sum(cf, axis=1, keepdims=True)
    carry_ref[...] = jnp.broadcast_to(total, carry_ref.shape)
    cnt_ref[...] = jnp.broadcast_to(total, cnt_ref.shape).astype(I32)


def route(logits, router_bias):
    E, T = logits.shape
    tt = min(512, T)
    trie = jnp.asarray(np.tril(np.ones((E, E), np.float32), -1), BF16)
    trit = jnp.asarray(np.triu(np.ones((tt, tt), np.float32), 1), BF16)
    kt = pl.BlockSpec((TOP_K, tt), lambda t: (0, t))
    return pl.pallas_call(
        _route_kernel,
        out_shape=(jax.ShapeDtypeStruct((TOP_K, T), I32), jax.ShapeDtypeStruct((TOP_K, T), I32),
                   jax.ShapeDtypeStruct((TOP_K, T), F32), jax.ShapeDtypeStruct((E, LANES), I32)),
        grid=(T // tt,),
        in_specs=[pl.BlockSpec((E, tt), lambda t: (0, t)),
                  pl.BlockSpec((E, 1), lambda t: (0, 0)),
                  pl.BlockSpec((E, E), lambda t: (0, 0)),
                  pl.BlockSpec((tt, tt), lambda t: (0, 0))],
        out_specs=(kt, kt, kt, pl.BlockSpec((E, LANES), lambda t: (0, 0))),
        scratch_shapes=[pltpu.VMEM((E, LANES), F32)],
        compiler_params=_params("arbitrary"),
        name="route",
    )(logits, router_bias.reshape(E, 1), trie, trit)


def _dest_kernel(off_ref, eid_ref, pos_ref, dest_ref):
    eid = eid_ref[...]
    acc = pos_ref[...]
    for e in range(N_EXPERTS):
        acc = acc + jnp.where(eid == e, off_ref[e], 0)
    dest_ref[...] = acc


def destination_rows(offsets, eid, pos):
    K, T = eid.shape
    tt = min(2048, T)
    kt = pl.BlockSpec((K, tt), lambda t, off: (0, t))
    return pl.pallas_call(
        _dest_kernel,
        out_shape=jax.ShapeDtypeStruct((K, T), I32),
        grid_spec=pltpu.PrefetchScalarGridSpec(num_scalar_prefetch=1, grid=(T // tt,),
                                               in_specs=[kt, kt], out_specs=kt),
        compiler_params=_params("parallel"),
        name="destination_rows",
    )(offsets, eid, pos)


def _new_expert(eid_ref):
    v = pl.program_id(0)
    return (v == 0) | (eid_ref[v] != eid_ref[jnp.maximum(v - 1, 0)])


def _write_rows(out_ref, vals, tile_ref, lo_ref, hi_ref, first_ref):
    v = pl.program_id(0)
    tm = out_ref.shape[0]
    row = tile_ref[v] * tm + lax.broadcasted_iota(I32, (tm, 1), 0)
    mine = (row >= lo_ref[v]) & (row < hi_ref[v])

    @pl.when(first_ref[v] == 1)
    def _():
        out_ref[...] = jnp.where(mine, vals, jnp.zeros_like(vals))

    @pl.when(first_ref[v] == 0)
    def _():
        out_ref[...] = jnp.where(mine, vals, out_ref[...])


def _experts_up_kernel(tile_ref, eid_ref, lo_ref, hi_ref, first_ref, tok_ref, nxt_ref, hp_hbm, wg_ref, wu_ref,
                       act_ref, xbuf, wg_s, wu_s, sem, *, n_tiles):
    @pl.when(_new_expert(eid_ref))
    def _():
        wg_s[...] = wg_ref[...].astype(BF16)
        wu_s[...] = wu_ref[...].astype(BF16)

    v = pl.program_id(0)
    tm, half = xbuf.shape[1:]
    tile = tile_ref[v]
    slot = tile % 2

    def row_copy(idx_ref, i, s):
        return pltpu.make_async_copy(hp_hbm.at[pl.ds(idx_ref[0, i], 1)], xbuf.at[s, pl.ds(i, 1)], sem.at[s])

    def gate_up(x):
        x_lo, x_hi = _unpack_pair(x)
        x_lo, x_hi = x_lo.astype(BF16), x_hi.astype(BF16)
        gate = (jnp.dot(x_lo, wg_s[:half, :], preferred_element_type=F32)
                + jnp.dot(x_hi, wg_s[half:, :], preferred_element_type=F32))
        up = (jnp.dot(x_lo, wu_s[:half, :], preferred_element_type=F32)
              + jnp.dot(x_hi, wu_s[half:, :], preferred_element_type=F32))
        return _silu(gate) * up

    nonempty = hi_ref[v] > lo_ref[v]
    row = tile * tm + lax.broadcasted_iota(I32, (tm, 1), 0)
    mine = (row >= lo_ref[v]) & (row < hi_ref[v])

    def first_visit(request_next):
        @pl.when(tile == 0)
        def _():
            def start(i, carry):
                row_copy(tok_ref, i, 0).start()
                return carry
            lax.fori_loop(0, tm, start, 0, unroll=8)

        def wait(i, carry):
            row_copy(tok_ref, i, slot).wait()
            return carry
        lax.fori_loop(0, tm, wait, 0, unroll=8)

        if request_next:
            for i in range(tm):
                row_copy(nxt_ref, i, 1 - slot).start(priority=i % 2)
        vals = gate_up(xbuf[slot])
        act_ref[...] = jnp.where(mine, vals, 0.0)

    is_first = nonempty & (first_ref[v] == 1)

    @pl.when(is_first & (tile + 1 < n_tiles))
    def _():
        first_visit(True)

    @pl.when(is_first & (tile + 1 >= n_tiles))
    def _():
        first_visit(False)

    @pl.when(nonempty & (first_ref[v] == 0))
    def _():
        vals = gate_up(xbuf[slot])
        act_ref[...] = jnp.where(mine, vals, act_ref[...])


def _experts_down_kernel(tile_ref, eid_ref, lo_ref, hi_ref, first_ref, act_ref, wd_ref, y_ref, wd_s):
    @pl.when(_new_expert(eid_ref))
    def _():
        wd_s[...] = wd_ref[...].astype(BF16)

    v = pl.program_id(0)

    @pl.when(hi_ref[v] > lo_ref[v])
    def _():
        half = y_ref.shape[1]
        act = act_ref[...].astype(BF16)
        packed = _pack_pair(jnp.dot(act, wd_s[:, :half], preferred_element_type=F32),
                            jnp.dot(act, wd_s[:, half:], preferred_element_type=F32))
        _write_rows(y_ref, packed, tile_ref, lo_ref, hi_ref, first_ref)


def expert_segments(counts, n_rows, tm):
    E = counts.shape[0]
    n_tiles = n_rows // tm
    ends = jnp.cumsum(counts)
    starts = ends - counts
    cuts = jnp.sort(jnp.concatenate([jnp.arange(n_tiles, dtype=I32) * tm, starts.astype(I32)]))
    lo = cuts
    hi = jnp.concatenate([cuts[1:], jnp.array([n_rows], I32)])
    tile = jnp.minimum(lo // tm, n_tiles - 1)
    eid = jnp.minimum(jnp.sum(ends[None, :] <= lo[:, None], axis=1), E - 1).astype(I32)
    first = ((lo % tm == 0) & (hi > lo)).astype(I32)
    return tile.astype(I32), eid, lo.astype(I32), hi.astype(I32), first


def routed_experts(hp, token_of_row, segments, w_gate, w_up, w_down, layer):
    _, half = hp.shape
    R = token_of_row.shape[1]
    _, E, D, F = w_gate.shape
    tm = min(EXPERT_TILE, R)
    n_tiles = R // tm
    n_visits = segments[0].shape[0]

    def tile_spec(width):
        return pl.BlockSpec((tm, width), lambda v, tile, eid, lo, hi, first: (tile[v], 0))

    def weight_spec(rows, cols):
        return pl.BlockSpec((None, None, rows, cols), lambda v, tile, eid, lo, hi, first: (layer, eid[v], 0, 0))

    act = pl.pallas_call(
        functools.partial(_experts_up_kernel, n_tiles=n_tiles),
        out_shape=jax.ShapeDtypeStruct((R, F), F32),
        grid_spec=pltpu.PrefetchScalarGridSpec(
            num_scalar_prefetch=5, grid=(n_visits,),
            in_specs=[pl.BlockSpec((1, tm), lambda v, tile, eid, lo, hi, first: (0, tile[v]),
                                   memory_space=pltpu.SMEM),
                      pl.BlockSpec((1, tm), lambda v, tile, eid, lo, hi, first:
                                   (0, jnp.minimum(tile[v] + 1, n_tiles - 1)), memory_space=pltpu.SMEM),
                      pl.BlockSpec(memory_space=pl.ANY),
                      weight_spec(D, F), weight_spec(D, F)],
            out_specs=tile_spec(F),
            scratch_shapes=[pltpu.VMEM((2, tm, half), U32),
                            pltpu.VMEM((D, F), BF16), pltpu.VMEM((D, F), BF16),
                            pltpu.SemaphoreType.DMA((2,))]),
        compiler_params=_params("arbitrary"),
        name="routed_experts_up",
    )(*segments, token_of_row, token_of_row, hp, w_gate, w_up)
    return pl.pallas_call(
        _experts_down_kernel,
        out_shape=jax.ShapeDtypeStruct((R, half), U32),
        grid_spec=pltpu.PrefetchScalarGridSpec(
            num_scalar_prefetch=5, grid=(n_visits,),
            in_specs=[tile_spec(F), weight_spec(F, D)],
            out_specs=tile_spec(half),
            scratch_shapes=[pltpu.VMEM((F, D), BF16)]),
        compiler_params=_params("arbitrary"),
        name="routed_experts_down",
    )(*segments, act, w_down)


def _combine_kernel(dest_ref, hp_ref, wt_ref, sg_ref, su_ref, sd_ref, x_ref, gate_ref, yp_hbm, out_ref,
                    buf, sg_s, su_s, sd_s, sem, *, tt):
    @pl.when(pl.program_id(0) == 0)
    def _():
        sg_s[...] = sg_ref[...].astype(BF16)
        su_s[...] = su_ref[...].astype(BF16)
        sd_s[...] = sd_ref[...].astype(BF16)

    def row_copy(i, k):
        return pltpu.make_async_copy(yp_hbm.at[pl.ds(dest_ref[k, i], 1)], buf.at[k, pl.ds(i, 1)], sem)

    def start(i, carry):
        for k in range(TOP_K):
            row_copy(i, k).start(priority=k % 2)
        return carry

    def wait(i, carry):
        for k in range(TOP_K):
            row_copy(i, k).wait()
        return carry

    lax.fori_loop(0, tt, start, 0)

    half = hp_ref.shape[1]
    h_lo, h_hi = _unpack_pair(hp_ref[...])
    h_lo, h_hi = h_lo.astype(BF16), h_hi.astype(BF16)
    g = (jnp.dot(h_lo, sg_s[:half, :], preferred_element_type=F32)
         + jnp.dot(h_hi, sg_s[half:, :], preferred_element_type=F32))
    u = (jnp.dot(h_lo, su_s[:half, :], preferred_element_type=F32)
         + jnp.dot(h_hi, su_s[half:, :], preferred_element_type=F32))
    act = (_silu(g) * u).astype(BF16)
    y_lo = jnp.dot(act, sd_s[:, :half], preferred_element_type=F32)
    y_hi = jnp.dot(act, sd_s[:, half:], preferred_element_type=F32)

    lax.fori_loop(0, tt, wait, 0)

    for k in range(TOP_K):
        e_lo, e_hi = _unpack_pair(buf[k])
        wk = wt_ref[:, k:k + 1]
        y_lo = y_lo + wk * e_lo
        y_hi = y_hi + wk * e_hi
    gate = gate_ref[...]
    out_ref[:, :half] = x_ref[:, :half] + gate[:, :half] * y_lo
    out_ref[:, half:] = x_ref[:, half:] + gate[:, half:] * y_hi


def combine_shared_residual(yp, dest, wt, hp, w_sg, w_su, w_sd, x, mod_rows, layer, which_gate):
    nb, S, D = x.shape
    T, half = hp.shape
    K = dest.shape[0]
    F = w_sg.shape[2]
    tt = min(128, S)
    nts = S // tt
    xrow = pl.BlockSpec((None, tt, D), lambda t: (t // nts, t % nts, 0))
    return pl.pallas_call(
        functools.partial(_combine_kernel, tt=tt),
        out_shape=jax.ShapeDtypeStruct((nb, S, D), F32),
        grid=(T // tt,),
        in_specs=[pl.BlockSpec((K, tt), lambda t: (0, t), memory_space=pltpu.SMEM),
                  pl.BlockSpec((tt, half), lambda t: (t, 0)),
                  pl.BlockSpec((tt, K), lambda t: (t, 0)),
                  pl.BlockSpec((None, D, F), lambda t: (layer, 0, 0)),
                  pl.BlockSpec((None, D, F), lambda t: (layer, 0, 0)),
                  pl.BlockSpec((None, F, D), lambda t: (layer, 0, 0)),
                  xrow,
                  pl.BlockSpec((None, 1, D), lambda t: ((layer * nb + t // nts) * N_MOD + which_gate, 0, 0)),
                  pl.BlockSpec(memory_space=pl.ANY)],
        out_specs=xrow,
        scratch_shapes=[pltpu.VMEM((K, tt, half), U32),
                        pltpu.VMEM((D, F), BF16), pltpu.VMEM((D, F), BF16), pltpu.VMEM((F, D), BF16),
                        pltpu.SemaphoreType.DMA(())],
        compiler_params=_params("arbitrary"),
        name="combine_shared_residual",
    )(dest, hp, wt, w_sg, w_su, w_sd, x, mod_rows, yp)


def _rmsnorm_kernel(x_ref, w_ref, out_ref):
    x = x_ref[...]
    out_ref[...] = x * lax.rsqrt(jnp.mean(x * x, axis=-1, keepdims=True) + EPS) * w_ref[...]


def rmsnorm(x, w):
    nb, S, D = x.shape
    ts = min(512, S)
    return pl.pallas_call(
        _rmsnorm_kernel,
        out_shape=jax.ShapeDtypeStruct((nb, S, D), F32),
        grid=(nb, S // ts),
        in_specs=[pl.BlockSpec((None, ts, D), lambda b, s: (b, s, 0)),
                  pl.BlockSpec((1, D), lambda b, s: (0, 0))],
        out_specs=pl.BlockSpec((None, ts, D), lambda b, s: (b, s, 0)),
        compiler_params=_params("parallel", "parallel"),
        name="final_rmsnorm",
    )(x, w.reshape(1, D))


def moe_block(x, norm_w, mod_rows, layer, w_router, router_bias, w_exp_gate, w_exp_up, w_exp_down,
              w_sh_gate, w_sh_up, w_sh_down):
    hp, logits = norm_router(x, norm_w, mod_rows, layer, w_router)
    eid, pos, wt, counts = route(logits, router_bias)
    counts = counts[:, 0]
    offsets = jnp.cumsum(counts) - counts
    dest = destination_rows(offsets, eid, pos)
    n_tok = hp.shape[0]
    token_of_row = (jnp.argsort(dest.reshape(-1)) % n_tok).astype(I32)[None, :]
    n_rows = token_of_row.shape[1]
    segments = expert_segments(counts, n_rows, min(EXPERT_TILE, n_rows))
    yp = routed_experts(hp, token_of_row, segments, w_exp_gate, w_exp_up, w_exp_down, layer)
    return combine_shared_residual(yp, dest, wt.T, hp, w_sh_gate, w_sh_up, w_sh_down, x, mod_rows, layer, 5)


def kernel(x, c, norm_mix, norm_ffn, w_ada, b_ada, w_in, hgrn_lower_bounds, hgrn_out_norm, w_proj_a, w_proj_b, w_out, w_router, router_bias, w_exp_gate, w_exp_up, w_exp_down, w_sh_gate, w_sh_up, w_sh_down, norm_final):
    nb, S, D = x.shape
    depth = w_in.shape[0]
    a_width = w_proj_a.shape[1]
    b_width = w_proj_b.shape[1]
    in_cols = w_in.shape[2]
    qkv_cols = 3 * a_width
    rest_cols = in_cols - qkv_cols
    col_qb = 0
    col_fb = col_qb + b_width
    col_ib = col_fb + b_width
    col_gb = col_ib + b_width
    col_gate_a = col_gb + b_width
    col_gate_b = col_gate_a + D
    dilations = tuple(d for _, d in DILATED_PATTERNS)
    assert all(w // d == SPAN for w, d in DILATED_PATTERNS)

    lb_sm = jax.nn.softmax(hgrn_lower_bounds.astype(F32), axis=0)
    lb_all = jnp.cumsum(lb_sm, axis=0) - lb_sm[0:1]
    cos, sin = rope_tables(S)

    mod = adaln_modulation(c, w_ada, b_ada)
    mod_rows = mod.reshape(depth * nb * N_MOD, 1, D)

    for l in range(depth):
        h = norm_modulate(x, norm_mix[:, None, :], mod_rows, l, 0, 1)
        qkvs = qkv_projection(h, w_in, l, qkv_cols, 2 * a_width, cos, sin, dilations)
        rest = matmul_cols(h.reshape(nb * S, D), w_in, l, qkv_cols, rest_cols, BF16)

        outs, lses = [], []
        for qkv in qkvs:
            o_g, lse_g = dilated_attention_branch(qkv, a_width)
            outs.append(o_g)
            lses.append(jnp.transpose(lse_g, (0, 2, 1, 3)).reshape(nb, S, -1))
        o_a = attention_merge(outs, lses, dilations)

        o_b = hgrn2(rest.reshape(nb, S, rest_cols), col_qb, col_fb, col_ib, col_gb, b_width,
                    lb_all[l][None, :], hgrn_out_norm[l][None, :])

        u = merge_projection(o_a.reshape(nb * S, a_width), o_b.reshape(nb * S, b_width),
                             w_proj_a, w_proj_b, l, rest, col_gate_a, col_gate_b)
        x = projection_residual(u.reshape(nb, S, D), w_out, x, mod_rows, l, 2)

        x = moe_block(x, norm_ffn[:, None, :], mod_rows, l, w_router[l], router_bias[l],
                      w_exp_gate, w_exp_up, w_exp_down, w_sh_gate, w_sh_up, w_sh_down)
    return rmsnorm(x, norm_final)
```
